```python
import math
import jax, jax.numpy as jnp
from jax import lax
import numpy as np

D_MODEL = 1024
BATCH = 8
SEQ = 4096
DEPTH = 2

N_MIXERS = 2
ROPE_THETA = 10000.0
NORM_EPS = 1e-6
Q_BLOCK = 128
D_FF = 2816
FFN_RESIDUAL_WEIGHT = 0.5

DIFF_HEADS = 8
DIFF_HEAD_DIM = D_MODEL // DIFF_HEADS // 2

MLA_HEADS = 8
MLA_NOPE = 128
MLA_ROPE = 64
MLA_V = 128
MLA_Q_RANK = 384
MLA_KV_RANK = 256

kernel_name = "hybrid_diffattn_mla_macaron"


def rmsnorm(x, gain):
    x32 = x.astype(jnp.float32)
    y = x32 * lax.rsqrt(jnp.mean(x32 * x32, axis=-1, keepdims=True) + NORM_EPS)
    return (y * gain.astype(jnp.float32)).astype(x.dtype)


def rope_tables(positions, dim):
    inv_freq = ROPE_THETA ** (-jnp.arange(0, dim, 2, dtype=jnp.float32) / dim)
    ang = positions.astype(jnp.float32)[..., None] * inv_freq
    return jnp.cos(ang), jnp.sin(ang)


def rope(x, cos, sin):
    shape = cos.shape[:2] + (1,) * (x.ndim - 3) + cos.shape[2:]
    c, s = cos.reshape(shape), sin.reshape(shape)
    x1, x2 = jnp.split(x.astype(jnp.float32), 2, axis=-1)
    return jnp.concatenate([x1 * c - x2 * s, x2 * c + x1 * s], axis=-1).astype(x.dtype)


def swiglu(h, w_gate, w_up, w_down):
    return (jax.nn.silu(h @ w_gate) * (h @ w_up)) @ w_down


def causal_softmax(scores, q_start, seq):
    q_pos = q_start + jnp.arange(Q_BLOCK)
    mask = jnp.arange(seq)[None, :] <= q_pos[:, None]
    return jax.nn.softmax(jnp.where(mask, scores, -jnp.inf), axis=-1)


def sweep_query_blocks(block_fn, q, seq):
    n_blocks = seq // Q_BLOCK
    def to_blocks(a):
        a = a.reshape((a.shape[0], n_blocks, Q_BLOCK) + a.shape[2:])
        return jnp.moveaxis(a, 1, 0)
    q_blocks = jax.tree_util.tree_map(to_blocks, q)
    starts = jnp.arange(n_blocks) * Q_BLOCK
    out = lax.map(lambda args: block_fn(args[0], args[1]), (q_blocks, starts))
    out = jnp.moveaxis(out, 0, 1)
    return out.reshape((out.shape[0], seq) + out.shape[3:])


def diff_attention(h, w_in, lq1, lk1, lq2, lk2, sub_gain, w_out, cos, sin, lambda_init):
    b, s, _ = h.shape
    q, k, v = jnp.split(h @ w_in, 3, axis=-1)
    q = rope(q.reshape(b, s, DIFF_HEADS, 2, DIFF_HEAD_DIM), cos, sin)
    k = rope(k.reshape(b, s, DIFF_HEADS, 2, DIFF_HEAD_DIM), cos, sin)
    v = v.reshape(b, s, DIFF_HEADS, 2 * DIFF_HEAD_DIM)
    f32 = jnp.float32
    lam = (jnp.exp(jnp.sum(lq1.astype(f32) * lk1.astype(f32)))
           - jnp.exp(jnp.sum(lq2.astype(f32) * lk2.astype(f32))) + lambda_init)
    scale = DIFF_HEAD_DIM ** -0.5

    def block(q_blk, start):
        sc = jnp.einsum('bqhcd,bkhcd->bhcqk', q_blk, k).astype(f32) * scale
        p = causal_softmax(sc, start, s)
        a = p[:, :, 0] - lam * p[:, :, 1]
        return jnp.einsum('bhqk,bkhe->bqhe', a.astype(v.dtype), v)

    o = sweep_query_blocks(block, q, s)
    o = rmsnorm(o, sub_gain) * (1.0 - lambda_init)
    return o.reshape(b, s, DIFF_HEADS * 2 * DIFF_HEAD_DIM) @ w_out


def mla_attention(h, w_in, q_norm, w_q_up, kv_norm, w_kv_up, w_out, cos, sin):
    b, s, _ = h.shape
    c = h @ w_in
    cq = c[..., :MLA_Q_RANK]
    ckv = c[..., MLA_Q_RANK:MLA_Q_RANK + MLA_KV_RANK]
    k_rope = rope(c[..., MLA_Q_RANK + MLA_KV_RANK:], cos, sin)
    q = (rmsnorm(cq, q_norm) @ w_q_up).reshape(b, s, MLA_HEADS, MLA_NOPE + MLA_ROPE)
    q_nope, q_rope = q[..., :MLA_NOPE], rope(q[..., MLA_NOPE:], cos, sin)
    kv = (rmsnorm(ckv, kv_norm) @ w_kv_up).reshape(b, s, MLA_HEADS, MLA_NOPE + MLA_V)
    k_nope, v = kv[..., :MLA_NOPE], kv[..., MLA_NOPE:]
    scale = (MLA_NOPE + MLA_ROPE) ** -0.5

    def block(q_blk, start):
        qn, qr = q_blk
        sc = (jnp.einsum('bqhd,bkhd->bhqk', qn, k_nope)
              + jnp.einsum('bqhr,bkr->bhqk', qr, k_rope)).astype(jnp.float32) * scale
        p = causal_softmax(sc, start, s)
        return jnp.einsum('bhqk,bkhd->bqhd', p.astype(v.dtype), v)

    o = sweep_query_blocks(block, (q_nope, q_rope), s)
    return o.reshape(b, s, MLA_HEADS * MLA_V) @ w_out


def setup_inputs(seed: int = 0) -> dict:
    key = jax.random.key(seed)
    ks = iter(jax.random.split(key, 40))
    n_diff = (DEPTH + 1) // 2
    n_mla = DEPTH // 2

    def w(shape, fan_in):
        return jax.random.normal(next(ks), shape, jnp.float32) * (fan_in ** -0.5)

    def gain(shape):
        return 1.0 + 0.02 * jax.random.normal(next(ks), shape, jnp.float32)

    x = jax.random.normal(next(ks), (BATCH, SEQ, D_MODEL), jnp.float32)
    offset = jax.random.randint(next(ks), (BATCH, 1), 0, 1024, dtype=jnp.int32)
    positions = offset + jnp.arange(SEQ, dtype=jnp.int32)[None, :]
    return {
        "x": x,
        "positions": positions,
        "ffn1_norm": gain((DEPTH, D_MODEL)),
        "ffn1_w_gate": w((DEPTH, D_MODEL, D_FF), D_MODEL),
        "ffn1_w_up": w((DEPTH, D_MODEL, D_FF), D_MODEL),
        "ffn1_w_down": w((DEPTH, D_FF, D_MODEL), D_FF),
        "mix_norm": gain((DEPTH, D_MODEL)),
        "ffn2_norm": gain((DEPTH, D_MODEL)),
        "ffn2_w_gate": w((DEPTH, D_MODEL, D_FF), D_MODEL),
        "ffn2_w_up": w((DEPTH, D_MODEL, D_FF), D_MODEL),
        "ffn2_w_down": w((DEPTH, D_FF, D_MODEL), D_FF),
        "diff_w_in": w((n_diff, D_MODEL, 3 * D_MODEL), D_MODEL),
        "diff_lambda_q1": 0.1 * jax.random.normal(next(ks), (n_diff, DIFF_HEAD_DIM), jnp.float32),
        "diff_lambda_k1": 0.1 * jax.random.normal(next(ks), (n_diff, DIFF_HEAD_DIM), jnp.float32),
        "diff_lambda_q2": 0.1 * jax.random.normal(next(ks), (n_diff, DIFF_HEAD_DIM), jnp.float32),
        "diff_lambda_k2": 0.1 * jax.random.normal(next(ks), (n_diff, DIFF_HEAD_DIM), jnp.float32),
        "diff_sub_norm": gain((n_diff, 2 * DIFF_HEAD_DIM)),
        "diff_w_out": w((n_diff, D_MODEL, D_MODEL), D_MODEL),
        "mla_w_in": w((n_mla, D_MODEL, MLA_Q_RANK + MLA_KV_RANK + MLA_ROPE), D_MODEL),
        "mla_q_norm": gain((n_mla, MLA_Q_RANK)),
        "mla_w_q_up": w((n_mla, MLA_Q_RANK, MLA_HEADS * (MLA_NOPE + MLA_ROPE)), MLA_Q_RANK),
        "mla_kv_norm": gain((n_mla, MLA_KV_RANK)),
        "mla_w_kv_up": w((n_mla, MLA_KV_RANK, MLA_HEADS * (MLA_NOPE + MLA_V)), MLA_KV_RANK),
        "mla_w_out": w((n_mla, MLA_HEADS * MLA_V, D_MODEL), MLA_HEADS * MLA_V),
        "final_norm": gain((D_MODEL,)),
    }


def reference(x, positions, ffn1_norm, ffn1_w_gate, ffn1_w_up, ffn1_w_down, mix_norm,
              ffn2_norm, ffn2_w_gate, ffn2_w_up, ffn2_w_down,
              diff_w_in, diff_lambda_q1, diff_lambda_k1, diff_lambda_q2, diff_lambda_k2,
              diff_sub_norm, diff_w_out,
              mla_w_in, mla_q_norm, mla_w_q_up, mla_kv_norm, mla_w_kv_up, mla_w_out,
              final_norm):
    cos_d, sin_d = rope_tables(positions, DIFF_HEAD_DIM)
    cos_m, sin_m = rope_tables(positions, MLA_ROPE)
    for i in range(DEPTH):
        x = x + FFN_RESIDUAL_WEIGHT * swiglu(rmsnorm(x, ffn1_norm[i]),
                                             ffn1_w_gate[i], ffn1_w_up[i], ffn1_w_down[i])
        h = rmsnorm(x, mix_norm[i])
        j = i // N_MIXERS
        if i % N_MIXERS == 0:
            lambda_init = 0.8 - 0.6 * math.exp(-0.3 * i)
            x = x + diff_attention(h, diff_w_in[j], diff_lambda_q1[j], diff_lambda_k1[j],
                                   diff_lambda_q2[j], diff_lambda_k2[j], diff_sub_norm[j],
                                   diff_w_out[j], cos_d, sin_d, lambda_init)
        else:
            x = x + mla_attention(h, mla_w_in[j], mla_q_norm[j], mla_w_q_up[j], mla_kv_norm[j],
                                  mla_w_kv_up[j], mla_w_out[j], cos_m, sin_m)
        x = x + FFN_RESIDUAL_WEIGHT * swiglu(rmsnorm(x, ffn2_norm[i]),
                                             ffn2_w_gate[i], ffn2_w_up[i], ffn2_w_down[i])
    return rmsnorm(x, final_norm)
```

```python
import functools
import math

import jax
import jax.numpy as jnp
from jax import lax
from jax.experimental import pallas as pl
from jax.experimental.pallas import tpu as pltpu

D_MODEL = 1024
D_FF = 2816
N_HEADS = 8
HEAD_W = 128
ROPE_DIM = 64
ROPE_THETA = 10000.0
NORM_EPS = 1e-6
FFN_RESIDUAL_WEIGHT = 0.5
MLA_NOPE = 128
MLA_ROPE = 64
MLA_Q_RANK = 384
MLA_KV_RANK = 256
MLA_QK_W = 256
DIFF_HEAD_DIM = 64

TOKEN_TILE = 256
ATTN_Q_TILE = 512
ATTN_K_TILE = 512
VMEM_LIMIT_BYTES = 56 * 1024 * 1024
MASK_VALUE = -1e30

F32 = jnp.float32
BF16 = jnp.bfloat16


def _dot(a, b):
    return jnp.dot(a, b, preferred_element_type=F32)


def _rms(x, gain):
    return x * lax.rsqrt(jnp.mean(x * x, axis=-1, keepdims=True) + NORM_EPS) * gain


def _ffn_residual(x, g_ref, wg_ref, wu_ref, wd_ref):
    h = _rms(x, g_ref[...]).astype(BF16)
    gate = _dot(h, wg_ref[...])
    up = _dot(h, wu_ref[...])
    act = (gate * jax.nn.sigmoid(gate) * up).astype(BF16)
    return x + FFN_RESIDUAL_WEIGHT * _dot(act, wd_ref[...])


def _rope(x, cos_f, sin_a, sin_b):
    return (x * cos_f + pltpu.roll(x, HEAD_W - ROPE_DIM // 2, axis=1) * sin_a
            + pltpu.roll(x, ROPE_DIM // 2, axis=1) * sin_b)


def _pre_diff_kernel(x_ref, cos_ref, sa_ref, sb_ref, fg_ref, wg_ref, wu_ref, wd_ref,
                     mg_ref, win_ref, xo_ref, q_ref, k_ref, v_ref):
    x = _ffn_residual(x_ref[0], fg_ref, wg_ref, wu_ref, wd_ref)
    xo_ref[0] = x
    h = _rms(x, mg_ref[...]).astype(BF16)
    qkv = _dot(h, win_ref[...])
    cos_f, sin_a, sin_b = cos_ref[0], sa_ref[0], sb_ref[0]
    scale = DIFF_HEAD_DIM ** -0.5
    for hd in range(N_HEADS):
        lo = hd * HEAD_W
        q = _rope(qkv[:, lo:lo + HEAD_W], cos_f, sin_a, sin_b) * scale
        k = _rope(qkv[:, D_MODEL + lo:D_MODEL + lo + HEAD_W], cos_f, sin_a, sin_b)
        q_ref[0, hd] = q.astype(BF16)
        k_ref[0, hd] = k.astype(BF16)
        v_ref[0, hd] = qkv[:, 2 * D_MODEL + lo:2 * D_MODEL + lo + HEAD_W].astype(BF16)


def _pre_mla_kernel(x_ref, cos_ref, sa_ref, sb_ref, fg_ref, wg_ref, wu_ref, wd_ref,
                    mg_ref, win_ref, qn_ref, wq_ref, kvn_ref, wkv_ref,
                    xo_ref, q_ref, k_ref, v_ref):
    x = _ffn_residual(x_ref[0], fg_ref, wg_ref, wu_ref, wd_ref)
    xo_ref[0] = x
    h = _rms(x, mg_ref[...]).astype(BF16)
    c = _dot(h, win_ref[...])
    cos_f, sin_a, sin_b = cos_ref[0], sa_ref[0], sb_ref[0]
    k_rope = _rope(c[:, MLA_Q_RANK + MLA_KV_RANK:], cos_f, sin_a, sin_b).astype(BF16)
    cq = _rms(c[:, :MLA_Q_RANK], qn_ref[...]).astype(BF16)
    ckv = _rms(c[:, MLA_Q_RANK:MLA_Q_RANK + MLA_KV_RANK], kvn_ref[...]).astype(BF16)
    q = _dot(cq, wq_ref[...])
    kv = _dot(ckv, wkv_ref[...])
    scale = (MLA_NOPE + MLA_ROPE) ** -0.5
    for hd in range(N_HEADS):
        lo = hd * MLA_QK_W
        q_ref[0, hd, :, :HEAD_W] = (q[:, lo:lo + HEAD_W] * scale).astype(BF16)
        q_rope = _rope(q[:, lo + HEAD_W:lo + 2 * HEAD_W], cos_f, sin_a, sin_b) * scale
        q_ref[0, hd, :, HEAD_W:] = q_rope.astype(BF16)
        k_ref[0, hd, :, :HEAD_W] = kv[:, lo:lo + HEAD_W].astype(BF16)
        k_ref[0, hd, :, HEAD_W:] = k_rope
        v_ref[0, hd] = kv[:, lo + HEAD_W:lo + 2 * HEAD_W].astype(BF16)


def _post_kernel(*refs, final_norm):
    if final_norm:
        x_ref, o_ref, wo_ref, fg_ref, wg_ref, wu_ref, wd_ref, ng_ref, xo_ref = refs
    else:
        x_ref, o_ref, wo_ref, fg_ref, wg_ref, wu_ref, wd_ref, xo_ref = refs
    o = jnp.concatenate([o_ref[0, hd] for hd in range(N_HEADS)], axis=1)
    x = x_ref[0] + _dot(o, wo_ref[...])
    x = _ffn_residual(x, fg_ref, wg_ref, wu_ref, wd_ref)
    if final_norm:
        x = _rms(x, ng_ref[...])
    xo_ref[0] = x


def _attn_kernel(*refs, tq, tk, diff, lambda_init):
    if diff:
        q_ref, k_ref, v_ref, lq1_ref, lk1_ref, lq2_ref, lk2_ref, sg_ref, o_ref = refs
    else:
        q_ref, k_ref, v_ref, o_ref = refs
    qi = pl.program_id(2)
    q = q_ref[0, 0]
    if diff:
        lane = lax.broadcasted_iota(jnp.int32, q.shape, 1)
        zero = jnp.zeros_like(q)
        q = jnp.concatenate([jnp.where(lane < DIFF_HEAD_DIM, q, zero),
                             jnp.where(lane >= DIFF_HEAD_DIM, q, zero)], axis=0)
    rows = q.shape[0]

    def tile(j, carry, diag):
        m, l, acc = carry
        start = pl.multiple_of(j * tk, tk)
        k = k_ref[0, 0, pl.ds(start, tk), :]
        v = v_ref[0, 0, pl.ds(start, tk), :]
        s = lax.dot_general(q, k, (((1,), (1,)), ((), ())), preferred_element_type=F32)
        if diag is not None:
            row = lax.broadcasted_iota(jnp.int32, (rows, tk), 0) & (tq - 1)
            col = lax.broadcasted_iota(jnp.int32, (rows, tk), 1) + diag * tk
            s = jnp.where(col <= row, s, MASK_VALUE)
        m_new = jnp.maximum(m, jnp.max(s, axis=1, keepdims=True))
        alpha = jnp.exp(m - m_new)
        p = jnp.exp(s - m_new)
        l = alpha * l + jnp.sum(p, axis=1, keepdims=True)
        acc = alpha * acc + _dot(p.astype(BF16), v)
        return m_new, l, acc

    tiles_per_q = tq // tk
    carry = (jnp.full((rows, 1), MASK_VALUE, F32), jnp.zeros((rows, 1), F32),
             jnp.zeros((rows, HEAD_W), F32))
    carry = lax.fori_loop(0, qi * tiles_per_q, lambda j, c: tile(j, c, None), carry)
    for d in range(tiles_per_q):
        carry = tile(qi * tiles_per_q + d, carry, d)
    _, l, acc = carry
    o = acc / l
    if diff:
        lam = (jnp.exp(jnp.sum(lq1_ref[...] * lk1_ref[...], axis=-1, keepdims=True))
               - jnp.exp(jnp.sum(lq2_ref[...] * lk2_ref[...], axis=-1, keepdims=True))
               + lambda_init)
        o = o[:tq] - lam * o[tq:]
        o = _rms(o, sg_ref[...]) * (1.0 - lambda_init)
    o_ref[0, 0] = o.astype(BF16)


def _const_spec(shape):
    zeros = (0,) * len(shape)
    return pl.BlockSpec(shape, lambda *_: zeros, pipeline_mode=pl.Buffered(1))


def _token_specs(b_s, widths):
    del b_s
    return [pl.BlockSpec((1, TOKEN_TILE, w), lambda b, i: (b, i, 0)) for w in widths]


def _head_spec(width):
    return pl.BlockSpec((1, N_HEADS, TOKEN_TILE, width), lambda b, i: (b, 0, i, 0))


def _params():
    return pltpu.CompilerParams(dimension_semantics=("arbitrary", "arbitrary"),
                                vmem_limit_bytes=VMEM_LIMIT_BYTES)


def _ffn_specs():
    return [_const_spec((1, D_MODEL)), _const_spec((D_MODEL, D_FF)), _const_spec((D_MODEL, D_FF)),
            _const_spec((D_FF, D_MODEL))]


def _pre_diff(x, tables, ffn, mix_g, w_in):
    b, s, _ = x.shape
    head = jax.ShapeDtypeStruct((b, N_HEADS, s, HEAD_W), BF16)
    return pl.pallas_call(
        _pre_diff_kernel,
        grid=(b, s // TOKEN_TILE),
        in_specs=_token_specs((b, s), [D_MODEL, HEAD_W, HEAD_W, HEAD_W]) + _ffn_specs()
        + [_const_spec((1, D_MODEL)), _const_spec((D_MODEL, 3 * D_MODEL))],
        out_specs=_token_specs((b, s), [D_MODEL]) + [_head_spec(HEAD_W)] * 3,
        out_shape=[jax.ShapeDtypeStruct(x.shape, F32), head, head, head],
        compiler_params=_params(),
        name="pre_diff",
    )(x, *tables, *ffn, mix_g, w_in)


def _pre_mla(x, tables, ffn, mix_g, w_in, q_norm, w_q, kv_norm, w_kv):
    b, s, _ = x.shape
    head = jax.ShapeDtypeStruct((b, N_HEADS, s, HEAD_W), BF16)
    head_qk = jax.ShapeDtypeStruct((b, N_HEADS, s, MLA_QK_W), BF16)
    return pl.pallas_call(
        _pre_mla_kernel,
        grid=(b, s // TOKEN_TILE),
        in_specs=_token_specs((b, s), [D_MODEL, HEAD_W, HEAD_W, HEAD_W]) + _ffn_specs()
        + [_const_spec((1, D_MODEL)), _const_spec(w_in.shape), _const_spec(q_norm.shape),
           _const_spec(w_q.shape), _const_spec(kv_norm.shape), _const_spec(w_kv.shape)],
        out_specs=_token_specs((b, s), [D_MODEL])
        + [_head_spec(MLA_QK_W), _head_spec(MLA_QK_W), _head_spec(HEAD_W)],
        out_shape=[jax.ShapeDtypeStruct(x.shape, F32), head_qk, head_qk, head],
        compiler_params=_params(),
        name="pre_mla",
    )(x, *tables, *ffn, mix_g, w_in, q_norm, w_q, kv_norm, w_kv)


def _post(x, o, w_out, ffn, final_gain=None):
    b, s, _ = x.shape
    final_norm = final_gain is not None
    extra = [final_gain] if final_norm else []
    return pl.pallas_call(
        functools.partial(_post_kernel, final_norm=final_norm),
        grid=(b, s // TOKEN_TILE),
        in_specs=_token_specs((b, s), [D_MODEL]) + [_head_spec(HEAD_W), _const_spec(w_out.shape)]
        + _ffn_specs() + [_const_spec((1, D_MODEL))] * len(extra),
        out_specs=_token_specs((b, s), [D_MODEL])[0],
        out_shape=jax.ShapeDtypeStruct(x.shape, F32),
        compiler_params=_params(),
        name="post_final" if final_norm else "post",
    )(x, o, w_out, *ffn, *extra)


def _attention(q, k, v, diff_params=None, lambda_init=0.0):
    b, h, s, dqk = q.shape
    tq, tk = ATTN_Q_TILE, ATTN_K_TILE
    diff = diff_params is not None
    extra = list(diff_params) if diff else []
    return pl.pallas_call(
        functools.partial(_attn_kernel, tq=tq, tk=tk, diff=diff, lambda_init=lambda_init),
        grid=(b, h, s // tq),
        in_specs=[pl.BlockSpec((1, 1, tq, dqk), lambda bi, hi, qi: (bi, hi, qi, 0)),
                  pl.BlockSpec((1, 1, s, dqk), lambda bi, hi, qi: (bi, hi, 0, 0)),
                  pl.BlockSpec((1, 1, s, HEAD_W), lambda bi, hi, qi: (bi, hi, 0, 0))]
        + [pl.BlockSpec(e.shape, lambda bi, hi, qi: (0, 0)) for e in extra],
        out_specs=pl.BlockSpec((1, 1, tq, HEAD_W), lambda bi, hi, qi: (bi, hi, qi, 0)),
        out_shape=jax.ShapeDtypeStruct((b, h, s, HEAD_W), BF16),
        compiler_params=pltpu.CompilerParams(
            dimension_semantics=("arbitrary", "arbitrary", "arbitrary"),
            vmem_limit_bytes=VMEM_LIMIT_BYTES),
        name="attn_diff" if diff else "attn_mla",
    )(q, k, v, *extra)


def _rope_tables(positions):
    inv_freq = ROPE_THETA ** (-jnp.arange(0, ROPE_DIM, 2, dtype=F32) / ROPE_DIM)
    ang = positions.astype(F32)[..., None] * inv_freq
    cos, sin = jnp.cos(ang), jnp.sin(ang)
    zero = jnp.zeros_like(sin)
    return (jnp.concatenate([cos, cos, cos, cos], axis=-1),
            jnp.concatenate([-sin, zero, -sin, zero], axis=-1),
            jnp.concatenate([zero, sin, zero, sin], axis=-1))


def _row(v):
    return v.reshape(1, -1).astype(F32)


def kernel(x, positions, ffn1_norm, ffn1_w_gate, ffn1_w_up, ffn1_w_down, mix_norm, ffn2_norm, ffn2_w_gate, ffn2_w_up, ffn2_w_down, diff_w_in, diff_lambda_q1, diff_lambda_k1, diff_lambda_q2, diff_lambda_k2, diff_sub_norm, diff_w_out, mla_w_in, mla_q_norm, mla_w_q_up, mla_kv_norm, mla_w_kv_up, mla_w_out, final_norm):
    depth = ffn1_norm.shape[0]
    assert depth == 2 and diff_w_in.shape[0] == 1 and mla_w_in.shape[0] == 1
    tables = _rope_tables(positions)

    def ffn(norm, wg, wu, wd, i):
        return (_row(norm[i]), wg[i].astype(BF16), wu[i].astype(BF16), wd[i].astype(BF16))

    ffn1 = [ffn(ffn1_norm, ffn1_w_gate, ffn1_w_up, ffn1_w_down, i) for i in range(depth)]
    ffn2 = [ffn(ffn2_norm, ffn2_w_gate, ffn2_w_up, ffn2_w_down, i) for i in range(depth)]

    lambda_init = 0.8 - 0.6 * math.exp(-0.3 * 0)
    x, q, k, v = _pre_diff(x, tables, ffn1[0], _row(mix_norm[0]), diff_w_in[0].astype(BF16))
    diff_params = (_row(diff_lambda_q1[0]), _row(diff_lambda_k1[0]), _row(diff_lambda_q2[0]),
                   _row(diff_lambda_k2[0]), _row(diff_sub_norm[0]))
    o = _attention(q, k, v, diff_params, lambda_init)
    x = _post(x, o, diff_w_out[0].astype(BF16), ffn2[0])

    w_in = jnp.pad(mla_w_in[0], ((0, 0), (0, HEAD_W - MLA_ROPE))).astype(BF16)
    w_q = mla_w_q_up[0].reshape(MLA_Q_RANK, N_HEADS, MLA_NOPE + MLA_ROPE)
    w_q = jnp.pad(w_q, ((0, 0), (0, 0), (0, MLA_QK_W - MLA_NOPE - MLA_ROPE)))
    w_q = w_q.reshape(MLA_Q_RANK, N_HEADS * MLA_QK_W).astype(BF16)
    x, q, k, v = _pre_mla(x, tables, ffn1[1], _row(mix_norm[1]), w_in, _row(mla_q_norm[0]), w_q,
                          _row(mla_kv_norm[0]), mla_w_kv_up[0].astype(BF16))
    o = _attention(q, k, v)
    return _post(x, o, mla_w_out[0].astype(BF16), ffn2[1], _row(final_norm))
```

```python
import functools
import math

import jax
import jax.numpy as jnp
from jax import lax
from jax.experimental import pallas as pl
from jax.experimental.pallas import tpu as pltpu

D_MODEL = 1024
D_FF = 2816
N_HEADS = 8
HEAD_W = 128
ROPE_DIM = 64
ROPE_THETA = 10000.0
NORM_EPS = 1e-6
FFN_RESIDUAL_WEIGHT = 0.5
MLA_NOPE = 128
MLA_ROPE = 64
MLA_Q_RANK = 384
MLA_KV_RANK = 256
MLA_QK_W = 256
DIFF_HEAD_DIM = 64

TOKEN_TILE = 256
ATTN_Q_TILE = 1024
ATTN_K_TILE = ATTN_Q_TILE // 2
VMEM_LIMIT_BYTES = 56 * 1024 * 1024
SOFTMAX_ROWS = 64
MASK_VALUE = -1e30
LOG2_E = math.log2(math.e)

F32 = jnp.float32
BF16 = jnp.bfloat16


def _dot(a, b):
    return jnp.dot(a, b, preferred_element_type=F32)


def _rms(x, gain):
    return x * lax.rsqrt(jnp.mean(x * x, axis=-1, keepdims=True) + NORM_EPS) * gain


def _ffn_residual(x, g_ref, wg_ref, wu_ref, wd_ref):
    h = _rms(x, g_ref[...]).astype(BF16)
    gate = _dot(h, wg_ref[...])
    up = _dot(h, wu_ref[...])
    act = (gate * jax.nn.sigmoid(gate) * up).astype(BF16)
    return x + FFN_RESIDUAL_WEIGHT * _dot(act, wd_ref[...])


def _rope(x, cos_f, sin_a, sin_b):
    return (x * cos_f + pltpu.roll(x, HEAD_W - ROPE_DIM // 2, axis=1) * sin_a
            + pltpu.roll(x, ROPE_DIM // 2, axis=1) * sin_b)


def _pre_diff_kernel(x_ref, cos_ref, sa_ref, sb_ref, fg_ref, wg_ref, wu_ref, wd_ref,
                     mg_ref, win_ref, xo_ref, q_ref, k_ref, v_ref):
    x = _ffn_residual(x_ref[0], fg_ref, wg_ref, wu_ref, wd_ref)
    xo_ref[0] = x
    h = _rms(x, mg_ref[...]).astype(BF16)
    qkv = _dot(h, win_ref[...])
    cos_f, sin_a, sin_b = cos_ref[0], sa_ref[0], sb_ref[0]
    scale = DIFF_HEAD_DIM ** -0.5 * LOG2_E
    for hd in range(N_HEADS):
        lo = hd * HEAD_W
        q = _rope(qkv[:, lo:lo + HEAD_W], cos_f, sin_a, sin_b) * scale
        k = _rope(qkv[:, D_MODEL + lo:D_MODEL + lo + HEAD_W], cos_f, sin_a, sin_b)
        q_ref[0, hd] = q.astype(BF16)
        k_ref[0, hd] = k.astype(BF16)
        v_ref[0, hd] = qkv[:, 2 * D_MODEL + lo:2 * D_MODEL + lo + HEAD_W].astype(BF16)


def _pre_mla_kernel(x_ref, cos_ref, sa_ref, sb_ref, fg_ref, wg_ref, wu_ref, wd_ref,
                    mg_ref, win_ref, qn_ref, wq_ref, kvn_ref, wkv_ref,
                    xo_ref, q_ref, k_ref, v_ref):
    x = _ffn_residual(x_ref[0], fg_ref, wg_ref, wu_ref, wd_ref)
    xo_ref[0] = x
    h = _rms(x, mg_ref[...]).astype(BF16)
    c = _dot(h, win_ref[...])
    cos_f, sin_a, sin_b = cos_ref[0], sa_ref[0], sb_ref[0]
    k_rope = _rope(c[:, MLA_Q_RANK + MLA_KV_RANK:], cos_f, sin_a, sin_b).astype(BF16)
    cq = _rms(c[:, :MLA_Q_RANK], qn_ref[...]).astype(BF16)
    ckv = _rms(c[:, MLA_Q_RANK:MLA_Q_RANK + MLA_KV_RANK], kvn_ref[...]).astype(BF16)
    q = _dot(cq, wq_ref[...])
    kv = _dot(ckv, wkv_ref[...])
    scale = (MLA_NOPE + MLA_ROPE) ** -0.5 * LOG2_E
    for hd in range(N_HEADS):
        lo = hd * MLA_QK_W
        q_ref[0, hd, :, :HEAD_W] = (q[:, lo:lo + HEAD_W] * scale).astype(BF16)
        q_rope = _rope(q[:, lo + HEAD_W:lo + 2 * HEAD_W], cos_f, sin_a, sin_b) * scale
        q_ref[0, hd, :, HEAD_W:] = q_rope.astype(BF16)
        k_ref[0, hd, :, :HEAD_W] = kv[:, lo:lo + HEAD_W].astype(BF16)
        k_ref[0, hd, :, HEAD_W:] = k_rope
        v_ref[0, hd] = kv[:, lo + HEAD_W:lo + 2 * HEAD_W].astype(BF16)


def _post_kernel(*refs, final_norm):
    if final_norm:
        x_ref, o_ref, wo_ref, fg_ref, wg_ref, wu_ref, wd_ref, ng_ref, xo_ref = refs
    else:
        x_ref, o_ref, wo_ref, fg_ref, wg_ref, wu_ref, wd_ref, xo_ref = refs
    o = jnp.concatenate([o_ref[0, hd] for hd in range(N_HEADS)], axis=1)
    x = x_ref[0] + _dot(o, wo_ref[...])
    x = _ffn_residual(x, fg_ref, wg_ref, wu_ref, wd_ref)
    if final_norm:
        x = _rms(x, ng_ref[...])
    xo_ref[0] = x


def _attn_kernel(*refs, tq, tk, diff, lambda_init):
    n_extra = 5 if diff else 0
    q_ref, k_ref, v_ref = refs[:3]
    extra = refs[3:3 + n_extra]
    o_ref = refs[3 + n_extra]
    s_a, s_b, p_a, p_b, al_a, al_b, m_ref, l_ref, acc_ref = refs[4 + n_extra:4 + n_extra + 9]
    assert tq == 2 * tk
    all_rows = (0, tq)
    upper_rows = (tk, tq)
    qi = pl.program_id(2)

    def key_slice(j):
        return pl.ds(pl.multiple_of(j * tk, tk), tk)

    def softmax(s_ref, p_ref, al_ref, rows, diag=None):
        for r0 in range(rows[0], rows[1], SOFTMAX_ROWS):
            blk = slice(r0, r0 + SOFTMAX_ROWS)
            s = s_ref[blk]
            if diag is not None:
                q0 = r0 - diag * tk
                assert q0 + SOFTMAX_ROWS > 0
                if q0 < tk - 1:
                    row = lax.broadcasted_iota(jnp.int32, s.shape, 0) + q0
                    col = lax.broadcasted_iota(jnp.int32, s.shape, 1)
                    s = jnp.where(col <= row, s, MASK_VALUE)
            m_old = m_ref[blk]
            m_blk = jnp.max(s, axis=1, keepdims=True)
            m_new = jnp.maximum(m_old, jnp.broadcast_to(m_blk, m_old.shape))
            alpha = jnp.exp2(m_old - m_new)
            p_lanes = None
            for c in range(0, tk, HEAD_W):
                p = jnp.exp2(s[:, c:c + HEAD_W] - m_new)
                p_ref[blk, c:c + HEAD_W] = p.astype(BF16)
                p_lanes = p if p_lanes is None else p_lanes + p
            l_ref[blk] = alpha * l_ref[blk] + p_lanes
            m_ref[blk] = m_new
            al_ref[blk] = alpha

    def weighted_values(p_ref, al_ref, j, rows):
        blk = slice(*rows)
        acc_ref[blk] = al_ref[blk] * acc_ref[blk] + _dot(p_ref[blk], v_ref[0, 0, key_slice(j), :])

    def attend(q_rows):
        def scores(s_ref, j, rows):
            s_ref[slice(*rows)] = lax.dot_general(
                q_rows(*rows), k_ref[0, 0, key_slice(j), :], (((1,), (1,)), ((), ())),
                preferred_element_type=F32)

        m_ref[...] = jnp.full(m_ref.shape, MASK_VALUE, F32)
        l_ref[...] = jnp.zeros(l_ref.shape, F32)
        acc_ref[...] = jnp.zeros(acc_ref.shape, F32)

        d0 = 2 * qi
        scores(s_a, d0 + 1, upper_rows)
        scores(s_b, d0, all_rows)
        softmax(s_a, p_a, al_a, upper_rows, diag=1)
        weighted_values(p_a, al_a, d0 + 1, upper_rows)
        softmax(s_b, p_b, al_b, all_rows, diag=0)
        scores(s_a, 0, all_rows)

        def pair(i, pending):
            scores(s_b, 2 * i + 1, all_rows)
            weighted_values(p_b, al_b, pending, all_rows)
            softmax(s_a, p_a, al_a, all_rows)
            scores(s_a, 2 * i + 2, all_rows)
            weighted_values(p_a, al_a, 2 * i, all_rows)
            softmax(s_b, p_b, al_b, all_rows)
            return 2 * i + 1

        pending = lax.fori_loop(0, qi, pair, d0)
        weighted_values(p_b, al_b, pending, all_rows)
        return acc_ref[...] / jnp.sum(l_ref[...], axis=1, keepdims=True)

    if diff:
        qs_ref, o0_ref = refs[4 + n_extra + 9:]
        lane = lax.broadcasted_iota(jnp.int32, (tq, HEAD_W), 1)

        def half(first):
            keep = (lane < DIFF_HEAD_DIM) if first else (lane >= DIFF_HEAD_DIM)
            qs_ref[...] = jnp.where(keep, q_ref[0, 0], jnp.zeros((tq, HEAD_W), BF16))
            return attend(lambda lo, hi: qs_ref[lo:hi])

        o0_ref[...] = half(True)
        o1 = half(False)
        lq1_ref, lk1_ref, lq2_ref, lk2_ref, sg_ref = extra
        lam = (jnp.exp(jnp.sum(lq1_ref[...] * lk1_ref[...], axis=-1, keepdims=True))
               - jnp.exp(jnp.sum(lq2_ref[...] * lk2_ref[...], axis=-1, keepdims=True))
               + lambda_init)
        o = _rms(o0_ref[...] - lam * o1, sg_ref[...]) * (1.0 - lambda_init)
    else:
        o = attend(lambda lo, hi: q_ref[0, 0, lo:hi])
    o_ref[0, 0] = o.astype(BF16)


def _const_spec(shape):
    zeros = (0,) * len(shape)
    return pl.BlockSpec(shape, lambda *_: zeros, pipeline_mode=pl.Buffered(1))


def _token_specs(b_s, widths):
    del b_s
    return [pl.BlockSpec((1, TOKEN_TILE, w), lambda b, i: (b, i, 0)) for w in widths]


def _head_spec(width):
    return pl.BlockSpec((1, N_HEADS, TOKEN_TILE, width), lambda b, i: (b, 0, i, 0))


def _params():
    return pltpu.CompilerParams(dimension_semantics=("arbitrary", "arbitrary"),
                                vmem_limit_bytes=VMEM_LIMIT_BYTES)


def _ffn_specs():
    return [_const_spec((1, D_MODEL)), _const_spec((D_MODEL, D_FF)), _const_spec((D_MODEL, D_FF)),
            _const_spec((D_FF, D_MODEL))]


def _pre_diff(x, tables, ffn, mix_g, w_in):
    b, s, _ = x.shape
    head = jax.ShapeDtypeStruct((b, N_HEADS, s, HEAD_W), BF16)
    return pl.pallas_call(
        _pre_diff_kernel,
        grid=(b, s // TOKEN_TILE),
        in_specs=_token_specs((b, s), [D_MODEL, HEAD_W, HEAD_W, HEAD_W]) + _ffn_specs()
        + [_const_spec((1, D_MODEL)), _const_spec((D_MODEL, 3 * D_MODEL))],
        out_specs=_token_specs((b, s), [D_MODEL]) + [_head_spec(HEAD_W)] * 3,
        out_shape=[jax.ShapeDtypeStruct(x.shape, F32), head, head, head],
        compiler_params=_params(),
        name="pre_diff",
    )(x, *tables, *ffn, mix_g, w_in)


def _pre_mla(x, tables, ffn, mix_g, w_in, q_norm, w_q, kv_norm, w_kv):
    b, s, _ = x.shape
    head = jax.ShapeDtypeStruct((b, N_HEADS, s, HEAD_W), BF16)
    head_qk = jax.ShapeDtypeStruct((b, N_HEADS, s, MLA_QK_W), BF16)
    return pl.pallas_call(
        _pre_mla_kernel,
        grid=(b, s // TOKEN_TILE),
        in_specs=_token_specs((b, s), [D_MODEL, HEAD_W, HEAD_W, HEAD_W]) + _ffn_specs()
        + [_const_spec((1, D_MODEL)), _const_spec(w_in.shape), _const_spec(q_norm.shape),
           _const_spec(w_q.shape), _const_spec(kv_norm.shape), _const_spec(w_kv.shape)],
        out_specs=_token_specs((b, s), [D_MODEL])
        + [_head_spec(MLA_QK_W), _head_spec(MLA_QK_W), _head_spec(HEAD_W)],
        out_shape=[jax.ShapeDtypeStruct(x.shape, F32), head_qk, head_qk, head],
        compiler_params=_params(),
        name="pre_mla",
    )(x, *tables, *ffn, mix_g, w_in, q_norm, w_q, kv_norm, w_kv)


def _post(x, o, w_out, ffn, final_gain=None):
    b, s, _ = x.shape
    final_norm = final_gain is not None
    extra = [final_gain] if final_norm else []
    return pl.pallas_call(
        functools.partial(_post_kernel, final_norm=final_norm),
        grid=(b, s // TOKEN_TILE),
        in_specs=_token_specs((b, s), [D_MODEL]) + [_head_spec(HEAD_W), _const_spec(w_out.shape)]
        + _ffn_specs() + [_const_spec((1, D_MODEL))] * len(extra),
        out_specs=_token_specs((b, s), [D_MODEL])[0],
        out_shape=jax.ShapeDtypeStruct(x.shape, F32),
        compiler_params=_params(),
        name="post_final" if final_norm else "post",
    )(x, o, w_out, *ffn, *extra)


def _attention(q, k, v, diff_params=None, lambda_init=0.0):
    b, h, s, dqk = q.shape
    tq, tk = ATTN_Q_TILE, ATTN_K_TILE
    diff = diff_params is not None
    extra = list(diff_params) if diff else []
    return pl.pallas_call(
        functools.partial(_attn_kernel, tq=tq, tk=tk, diff=diff, lambda_init=lambda_init),
        grid=(b, h, s // tq),
        in_specs=[pl.BlockSpec((1, 1, tq, dqk), lambda bi, hi, qi: (bi, hi, qi, 0)),
                  pl.BlockSpec((1, 1, s, dqk), lambda bi, hi, qi: (bi, hi, 0, 0)),
                  pl.BlockSpec((1, 1, s, HEAD_W), lambda bi, hi, qi: (bi, hi, 0, 0))]
        + [pl.BlockSpec(e.shape, lambda bi, hi, qi: (0, 0)) for e in extra],
        out_specs=pl.BlockSpec((1, 1, tq, HEAD_W), lambda bi, hi, qi: (bi, hi, qi, 0)),
        out_shape=jax.ShapeDtypeStruct((b, h, s, HEAD_W), BF16),
        scratch_shapes=[pltpu.VMEM((tq, tk), F32), pltpu.VMEM((tq, tk), F32),
                        pltpu.VMEM((tq, tk), BF16), pltpu.VMEM((tq, tk), BF16)]
        + [pltpu.VMEM((tq, HEAD_W), F32) for _ in range(5)]
        + ([pltpu.VMEM((tq, dqk), BF16), pltpu.VMEM((tq, HEAD_W), F32)] if diff else []),
        compiler_params=pltpu.CompilerParams(
            dimension_semantics=("arbitrary", "arbitrary", "arbitrary"),
            vmem_limit_bytes=VMEM_LIMIT_BYTES),
        name="attn_diff" if diff else "attn_mla",
    )(q, k, v, *extra)


def _rope_tables(positions):
    inv_freq = ROPE_THETA ** (-jnp.arange(0, ROPE_DIM, 2, dtype=F32) / ROPE_DIM)
    ang = positions.astype(F32)[..., None] * inv_freq
    cos, sin = jnp.cos(ang), jnp.sin(ang)
    zero = jnp.zeros_like(sin)
    return (jnp.concatenate([cos, cos, cos, cos], axis=-1),
            jnp.concatenate([-sin, zero, -sin, zero], axis=-1),
            jnp.concatenate([zero, sin, zero, sin], axis=-1))


def _row(v):
    return v.reshape(1, -1).astype(F32)


def kernel(x, positions, ffn1_norm, ffn1_w_gate, ffn1_w_up, ffn1_w_down, mix_norm, ffn2_norm, ffn2_w_gate, ffn2_w_up, ffn2_w_down, diff_w_in, diff_lambda_q1, diff_lambda_k1, diff_lambda_q2, diff_lambda_k2, diff_sub_norm, diff_w_out, mla_w_in, mla_q_norm, mla_w_q_up, mla_kv_norm, mla_w_kv_up, mla_w_out, final_norm):
    depth = ffn1_norm.shape[0]
    assert depth == 2 and diff_w_in.shape[0] == 1 and mla_w_in.shape[0] == 1
    tables = _rope_tables(positions)

    def ffn(norm, wg, wu, wd, i):
        return (_row(norm[i]), wg[i].astype(BF16), wu[i].astype(BF16), wd[i].astype(BF16))

    ffn1 = [ffn(ffn1_norm, ffn1_w_gate, ffn1_w_up, ffn1_w_down, i) for i in range(depth)]
    ffn2 = [ffn(ffn2_norm, ffn2_w_gate, ffn2_w_up, ffn2_w_down, i) for i in range(depth)]

    lambda_init = 0.8 - 0.6 * math.exp(-0.3 * 0)
    x, q, k, v = _pre_diff(x, tables, ffn1[0], _row(mix_norm[0]), diff_w_in[0].astype(BF16))
    diff_params = (_row(diff_lambda_q1[0]), _row(diff_lambda_k1[0]), _row(diff_lambda_q2[0]),
                   _row(diff_lambda_k2[0]), _row(diff_sub_norm[0]))
    o = _attention(q, k, v, diff_params, lambda_init)
    x = _post(x, o, diff_w_out[0].astype(BF16), ffn2[0])

    w_in = jnp.pad(mla_w_in[0], ((0, 0), (0, HEAD_W - MLA_ROPE))).astype(BF16)
    w_q = mla_w_q_up[0].reshape(MLA_Q_RANK, N_HEADS, MLA_NOPE + MLA_ROPE)
    w_q = jnp.pad(w_q, ((0, 0), (0, 0), (0, MLA_QK_W - MLA_NOPE - MLA_ROPE)))
    w_q = w_q.reshape(MLA_Q_RANK, N_HEADS * MLA_QK_W).astype(BF16)
    x, q, k, v = _pre_mla(x, tables, ffn1[1], _row(mix_norm[1]), w_in, _row(mla_q_norm[0]), w_q,
                          _row(mla_kv_norm[0]), mla_w_kv_up[0].astype(BF16))
    o = _attention(q, k, v)
    return _post(x, o, mla_w_out[0].astype(BF16), ffn2[1], _row(final_norm))
```

```python
import functools
import math

import jax
import jax.numpy as jnp
from jax import lax
from jax.experimental import pallas as pl
from jax.experimental.pallas import tpu as pltpu

D_MODEL = 1024
D_FF = 2816
N_HEADS = 8
HEAD_W = 128
ROPE_DIM = 64
ROPE_THETA = 10000.0
NORM_EPS = 1e-6
FFN_RESIDUAL_WEIGHT = 0.5
MLA_NOPE = 128
MLA_ROPE = 64
MLA_Q_RANK = 384
MLA_KV_RANK = 256
MLA_QK_W = 256
DIFF_HEAD_DIM = 64

TOKEN_TILE = 256
ATTN_Q_TILE = 1024
ATTN_K_TILE = ATTN_Q_TILE // 2
VMEM_LIMIT_BYTES = 56 * 1024 * 1024
SOFTMAX_ROWS = 64
MXU_COLS = 256
MASK_VALUE = -1e30
LOG2_E = math.log2(math.e)

F32 = jnp.float32
BF16 = jnp.bfloat16


def _dot(a, b):
    return jnp.dot(a, b, preferred_element_type=F32)


def _rms(x, gain):
    return x * lax.rsqrt(jnp.mean(x * x, axis=-1, keepdims=True) + NORM_EPS) * gain


def _ffn_residual(x, g_ref, wg_ref, wu_ref, wd_ref):
    h = _rms(x, g_ref[...]).astype(BF16)
    gate = _dot(h, wg_ref[...])
    up = _dot(h, wu_ref[...])
    act = (gate * jax.nn.sigmoid(gate) * up).astype(BF16)
    return x + FFN_RESIDUAL_WEIGHT * _dot(act, wd_ref[...])


def _rope(x, cos_f, sin_a, sin_b):
    return (x * cos_f + pltpu.roll(x, HEAD_W - ROPE_DIM // 2, axis=1) * sin_a
            + pltpu.roll(x, ROPE_DIM // 2, axis=1) * sin_b)


def _rope_tables(cs):
    cos, sin = cs[:, :ROPE_DIM // 2], cs[:, ROPE_DIM // 2:]
    zero = jnp.zeros_like(sin)
    return (jnp.concatenate([cos, cos, cos, cos], axis=1),
            jnp.concatenate([-sin, zero, -sin, zero], axis=1),
            jnp.concatenate([zero, sin, zero, sin], axis=1))


def _pre_diff_kernel(x_ref, cs_ref, fg_ref, wg_ref, wu_ref, wd_ref,
                     mg_ref, win_ref, xo_ref, q_ref, k_ref, vt_ref):
    x = _ffn_residual(x_ref[0], fg_ref, wg_ref, wu_ref, wd_ref)
    xo_ref[0] = x
    h = _rms(x, mg_ref[...]).astype(BF16)
    qkv = _dot(h, win_ref[...])
    cos_f, sin_a, sin_b = _rope_tables(cs_ref[0])
    scale = DIFF_HEAD_DIM ** -0.5 * LOG2_E
    for hd in range(N_HEADS):
        lo = hd * HEAD_W
        q = _rope(qkv[:, lo:lo + HEAD_W], cos_f, sin_a, sin_b) * scale
        k = _rope(qkv[:, D_MODEL + lo:D_MODEL + lo + HEAD_W], cos_f, sin_a, sin_b)
        q_ref[0, hd] = q.astype(BF16)
        k_ref[0, hd] = k.astype(BF16)
        vt_ref[0, hd, 0] = qkv[:, 2 * D_MODEL + lo:2 * D_MODEL + lo + HEAD_W].T.astype(BF16)


def _pre_mla_kernel(x_ref, cs_ref, fg_ref, wg_ref, wu_ref, wd_ref,
                    mg_ref, win_ref, qn_ref, wq_ref, kvn_ref, wkv_ref,
                    xo_ref, q_ref, k_ref, vt_ref):
    x = _ffn_residual(x_ref[0], fg_ref, wg_ref, wu_ref, wd_ref)
    xo_ref[0] = x
    h = _rms(x, mg_ref[...]).astype(BF16)
    c = _dot(h, win_ref[...])
    cos_f, sin_a, sin_b = _rope_tables(cs_ref[0])
    k_rope = _rope(c[:, MLA_Q_RANK + MLA_KV_RANK:], cos_f, sin_a, sin_b).astype(BF16)
    cq = _rms(c[:, :MLA_Q_RANK], qn_ref[...]).astype(BF16)
    ckv = _rms(c[:, MLA_Q_RANK:MLA_Q_RANK + MLA_KV_RANK], kvn_ref[...]).astype(BF16)
    q = _dot(cq, wq_ref[...])
    kv = _dot(ckv, wkv_ref[...])
    scale = (MLA_NOPE + MLA_ROPE) ** -0.5 * LOG2_E
    for hd in range(N_HEADS):
        lo = hd * MLA_QK_W
        q_ref[0, hd, :, :HEAD_W] = (q[:, lo:lo + HEAD_W] * scale).astype(BF16)
        q_rope = _rope(q[:, lo + HEAD_W:lo + 2 * HEAD_W], cos_f, sin_a, sin_b) * scale
        q_ref[0, hd, :, HEAD_W:] = q_rope.astype(BF16)
        k_ref[0, hd, :, :HEAD_W] = kv[:, lo:lo + HEAD_W].astype(BF16)
        k_ref[0, hd, :, HEAD_W:] = k_rope
        vt_ref[0, hd, 0] = kv[:, lo + HEAD_W:lo + 2 * HEAD_W].T.astype(BF16)


def _post_kernel(*refs, final_norm):
    if final_norm:
        x_ref, o_ref, wo_ref, fg_ref, wg_ref, wu_ref, wd_ref, ng_ref, xo_ref = refs
    else:
        x_ref, o_ref, wo_ref, fg_ref, wg_ref, wu_ref, wd_ref, xo_ref = refs
    o = jnp.concatenate([o_ref[0, hd] for hd in range(N_HEADS)], axis=1)
    x = x_ref[0] + _dot(o, wo_ref[...])
    x = _ffn_residual(x, fg_ref, wg_ref, wu_ref, wd_ref)
    if final_norm:
        x = _rms(x, ng_ref[...])
    xo_ref[0] = x


def _attn_kernel(*refs, tq, tk, diff, lambda_init):
    n_extra = 5 if diff else 0
    q_ref, k_ref, vt_ref = refs[:3]
    extra = refs[3:3 + n_extra]
    o_ref = refs[3 + n_extra]
    s_a, s_b, p_a, p_b, al_a, al_b, m_ref, l_ref, acc_ref = refs[4 + n_extra:4 + n_extra + 9]
    assert tq == 2 * tk and tk % TOKEN_TILE == 0
    all_cols = (0, tq)
    upper_cols = (tk, tq)
    v_chunks = tk // TOKEN_TILE
    qi = pl.program_id(2)

    def key_slice(j):
        return pl.ds(pl.multiple_of(j * tk, tk), tk)

    def sublane_groups(x):
        return x.reshape(x.shape[0] // 8, 8, x.shape[1])

    def softmax(s_ref, p_ref, al_ref, cols, diag=None):
        for c0 in range(cols[0], cols[1], HEAD_W):
            cb = slice(c0, c0 + HEAD_W)
            off = None if diag is None else c0 - diag * tk
            assert off is None or off + HEAD_W > 0
            if off is not None and off >= tk - 1:
                off = None

            def load(r0):
                s = s_ref[r0:r0 + SOFTMAX_ROWS, cb]
                if off is not None and r0 + SOFTMAX_ROWS - 1 > off:
                    row = lax.broadcasted_iota(jnp.int32, s.shape, 0) + r0
                    col = lax.broadcasted_iota(jnp.int32, s.shape, 1) + off
                    s = jnp.where(row <= col, s, MASK_VALUE)
                return s

            live = tk if off is None else min(tk, -(-(off + HEAD_W) // SOFTMAX_ROWS) * SOFTMAX_ROWS)
            if live < tk:
                p_ref[live:, cb] = jnp.zeros((tk - live, HEAD_W), BF16)
            m8 = None
            for r0 in range(0, live, SOFTMAX_ROWS):
                part = jnp.max(sublane_groups(load(r0)), axis=0)
                m8 = part if m8 is None else jnp.maximum(m8, part)
            m_old = m_ref[:, cb]
            m_new = jnp.maximum(m_old, jnp.max(m8, axis=0, keepdims=True))
            alpha = jnp.exp2(m_old - m_new)
            l8 = alpha * l_ref[:, cb]
            for r0 in range(0, live, SOFTMAX_ROWS):
                p = jnp.exp2(load(r0) - m_new)
                p_ref[r0:r0 + SOFTMAX_ROWS, cb] = p.astype(BF16)
                l8 = l8 + jnp.sum(sublane_groups(p), axis=0)
            l_ref[:, cb] = l8
            m_ref[:, cb] = m_new
            al_ref[:, cb] = alpha

    def weighted_values(p_ref, al_ref, j, cols):
        for c0 in range(cols[0], cols[1], MXU_COLS):
            cs = slice(c0, c0 + MXU_COLS)
            pv = None
            for h in range(v_chunks):
                part = _dot(vt_ref[0, 0, j * v_chunks + h],
                            p_ref[h * TOKEN_TILE:(h + 1) * TOKEN_TILE, cs])
                pv = part if pv is None else pv + part
            acc_ref[:, cs] = al_ref[:, cs] * acc_ref[:, cs] + pv

    def attend(q_rows):
        def scores(s_ref, j, cols):
            k = k_ref[0, 0, key_slice(j), :]
            for c0 in range(cols[0], cols[1], MXU_COLS):
                s_ref[:, c0:c0 + MXU_COLS] = lax.dot_general(
                    k, q_rows(c0, c0 + MXU_COLS), (((1,), (1,)), ((), ())),
                    preferred_element_type=F32)

        m_ref[...] = jnp.full(m_ref.shape, MASK_VALUE, F32)
        l_ref[...] = jnp.zeros(l_ref.shape, F32)
        acc_ref[...] = jnp.zeros(acc_ref.shape, F32)

        d0 = 2 * qi
        scores(s_a, d0 + 1, upper_cols)
        scores(s_b, d0, all_cols)
        softmax(s_a, p_a, al_a, upper_cols, diag=1)
        weighted_values(p_a, al_a, d0 + 1, upper_cols)
        softmax(s_b, p_b, al_b, all_cols, diag=0)
        scores(s_a, 0, all_cols)

        def pair(i, pending):
            scores(s_b, 2 * i + 1, all_cols)
            weighted_values(p_b, al_b, pending, all_cols)
            softmax(s_a, p_a, al_a, all_cols)
            scores(s_a, 2 * i + 2, all_cols)
            weighted_values(p_a, al_a, 2 * i, all_cols)
            softmax(s_b, p_b, al_b, all_cols)
            return 2 * i + 1

        pending = lax.fori_loop(0, qi, pair, d0)
        weighted_values(p_b, al_b, pending, all_cols)
        return acc_ref[...] / jnp.sum(l_ref[...], axis=0, keepdims=True)

    if diff:
        qs_ref, o0_ref = refs[4 + n_extra + 9:]
        lane = lax.broadcasted_iota(jnp.int32, (tq, HEAD_W), 1)

        def half(first):
            keep = (lane < DIFF_HEAD_DIM) if first else (lane >= DIFF_HEAD_DIM)
            qs_ref[...] = jnp.where(keep, q_ref[0, 0], jnp.zeros((tq, HEAD_W), BF16))
            return attend(lambda lo, hi: qs_ref[lo:hi])

        o0_ref[...] = half(True)
        o1 = half(False)
        lq1_ref, lk1_ref, lq2_ref, lk2_ref, sg_ref = extra
        lam = (jnp.exp(jnp.sum(lq1_ref[...] * lk1_ref[...], axis=-1, keepdims=True))
               - jnp.exp(jnp.sum(lq2_ref[...] * lk2_ref[...], axis=-1, keepdims=True))
               + lambda_init)
        o = o0_ref[...] - lam * o1
        o = o * lax.rsqrt(jnp.mean(o * o, axis=0, keepdims=True) + NORM_EPS)
        o = o * (sg_ref[...] * (1.0 - lambda_init))
    else:
        o = attend(lambda lo, hi: q_ref[0, 0, lo:hi])
    o_ref[0, 0] = o.T.astype(BF16)


def _const_spec(shape):
    zeros = (0,) * len(shape)
    return pl.BlockSpec(shape, lambda *_: zeros, pipeline_mode=pl.Buffered(1))


def _token_specs(b_s, widths):
    del b_s
    return [pl.BlockSpec((1, TOKEN_TILE, w), lambda b, i: (b, i, 0)) for w in widths]


def _head_spec(width):
    return pl.BlockSpec((1, N_HEADS, TOKEN_TILE, width), lambda b, i: (b, 0, i, 0))


def _vt_shape(b, s):
    return jax.ShapeDtypeStruct((b, N_HEADS, s // TOKEN_TILE, HEAD_W, TOKEN_TILE), BF16)


def _vt_spec():
    return pl.BlockSpec((1, N_HEADS, 1, HEAD_W, TOKEN_TILE), lambda b, i: (b, 0, i, 0, 0))


def _params():
    return pltpu.CompilerParams(dimension_semantics=("arbitrary", "arbitrary"),
                                vmem_limit_bytes=VMEM_LIMIT_BYTES)


def _ffn_specs():
    return [_const_spec((1, D_MODEL)), _const_spec((D_MODEL, D_FF)), _const_spec((D_MODEL, D_FF)),
            _const_spec((D_FF, D_MODEL))]


def _pre_diff(x, phases, ffn, mix_g, w_in):
    b, s, _ = x.shape
    head = jax.ShapeDtypeStruct((b, N_HEADS, s, HEAD_W), BF16)
    return pl.pallas_call(
        _pre_diff_kernel,
        grid=(b, s // TOKEN_TILE),
        in_specs=_token_specs((b, s), [D_MODEL, ROPE_DIM]) + _ffn_specs()
        + [_const_spec((1, D_MODEL)), _const_spec((D_MODEL, 3 * D_MODEL))],
        out_specs=_token_specs((b, s), [D_MODEL]) + [_head_spec(HEAD_W)] * 2 + [_vt_spec()],
        out_shape=[jax.ShapeDtypeStruct(x.shape, F32), head, head, _vt_shape(b, s)],
        compiler_params=_params(),
        name="pre_diff",
    )(x, phases, *ffn, mix_g, w_in)


def _pre_mla(x, phases, ffn, mix_g, w_in, q_norm, w_q, kv_norm, w_kv):
    b, s, _ = x.shape
    head_qk = jax.ShapeDtypeStruct((b, N_HEADS, s, MLA_QK_W), BF16)
    return pl.pallas_call(
        _pre_mla_kernel,
        grid=(b, s // TOKEN_TILE),
        in_specs=_token_specs((b, s), [D_MODEL, ROPE_DIM]) + _ffn_specs()
        + [_const_spec((1, D_MODEL)), _const_spec(w_in.shape), _const_spec(q_norm.shape),
           _const_spec(w_q.shape), _const_spec(kv_norm.shape), _const_spec(w_kv.shape)],
        out_specs=_token_specs((b, s), [D_MODEL])
        + [_head_spec(MLA_QK_W), _head_spec(MLA_QK_W), _vt_spec()],
        out_shape=[jax.ShapeDtypeStruct(x.shape, F32), head_qk, head_qk, _vt_shape(b, s)],
        compiler_params=_params(),
        name="pre_mla",
    )(x, phases, *ffn, mix_g, w_in, q_norm, w_q, kv_norm, w_kv)


def _post(x, o, w_out, ffn, final_gain=None):
    b, s, _ = x.shape
    final_norm = final_gain is not None
    extra = [final_gain] if final_norm else []
    return pl.pallas_call(
        functools.partial(_post_kernel, final_norm=final_norm),
        grid=(b, s // TOKEN_TILE),
        in_specs=_token_specs((b, s), [D_MODEL]) + [_head_spec(HEAD_W), _const_spec(w_out.shape)]
        + _ffn_specs() + [_const_spec((1, D_MODEL))] * len(extra),
        out_specs=_token_specs((b, s), [D_MODEL])[0],
        out_shape=jax.ShapeDtypeStruct(x.shape, F32),
        compiler_params=_params(),
        name="post_final" if final_norm else "post",
    )(x, o, w_out, *ffn, *extra)


def _attention(q, k, vt, diff_params=None, lambda_init=0.0):
    b, h, s, dqk = q.shape
    tq, tk = ATTN_Q_TILE, ATTN_K_TILE
    diff = diff_params is not None
    extra = list(diff_params) if diff else []
    return pl.pallas_call(
        functools.partial(_attn_kernel, tq=tq, tk=tk, diff=diff, lambda_init=lambda_init),
        grid=(b, h, s // tq),
        in_specs=[pl.BlockSpec((1, 1, tq, dqk), lambda bi, hi, qi: (bi, hi, qi, 0)),
                  pl.BlockSpec((1, 1, s, dqk), lambda bi, hi, qi: (bi, hi, 0, 0)),
                  pl.BlockSpec((1, 1) + vt.shape[2:], lambda bi, hi, qi: (bi, hi, 0, 0, 0))]
        + [pl.BlockSpec(e.shape, lambda bi, hi, qi: (0, 0)) for e in extra],
        out_specs=pl.BlockSpec((1, 1, tq, HEAD_W), lambda bi, hi, qi: (bi, hi, qi, 0)),
        out_shape=jax.ShapeDtypeStruct((b, h, s, HEAD_W), BF16),
        scratch_shapes=[pltpu.VMEM((tk, tq), F32), pltpu.VMEM((tk, tq), F32),
                        pltpu.VMEM((tk, tq), BF16), pltpu.VMEM((tk, tq), BF16)]
        + [pltpu.VMEM((1, tq), F32) for _ in range(3)]
        + [pltpu.VMEM((8, tq), F32), pltpu.VMEM((HEAD_W, tq), F32)]
        + ([pltpu.VMEM((tq, dqk), BF16), pltpu.VMEM((HEAD_W, tq), F32)] if diff else []),
        compiler_params=pltpu.CompilerParams(
            dimension_semantics=("arbitrary", "arbitrary", "arbitrary"),
            vmem_limit_bytes=VMEM_LIMIT_BYTES),
        name="attn_diff" if diff else "attn_mla",
    )(q, k, vt, *extra)


def _rope_phases(positions):
    inv_freq = ROPE_THETA ** (-jnp.arange(0, ROPE_DIM, 2, dtype=F32) / ROPE_DIM)
    ang = positions.astype(F32)[..., None] * inv_freq
    return jnp.concatenate([jnp.cos(ang), jnp.sin(ang)], axis=-1)


def _row(v):
    return v.reshape(1, -1).astype(F32)


def kernel(x, positions, ffn1_norm, ffn1_w_gate, ffn1_w_up, ffn1_w_down, mix_norm, ffn2_norm, ffn2_w_gate, ffn2_w_up, ffn2_w_down, diff_w_in, diff_lambda_q1, diff_lambda_k1, diff_lambda_q2, diff_lambda_k2, diff_sub_norm, diff_w_out, mla_w_in, mla_q_norm, mla_w_q_up, mla_kv_norm, mla_w_kv_up, mla_w_out, final_norm):
    depth = ffn1_norm.shape[0]
    assert depth == 2 and diff_w_in.shape[0] == 1 and mla_w_in.shape[0] == 1
    phases = _rope_phases(positions)

    def ffn(norm, wg, wu, wd, i):
        return (_row(norm[i]), wg[i].astype(BF16), wu[i].astype(BF16), wd[i].astype(BF16))

    ffn1 = [ffn(ffn1_norm, ffn1_w_gate, ffn1_w_up, ffn1_w_down, i) for i in range(depth)]
    ffn2 = [ffn(ffn2_norm, ffn2_w_gate, ffn2_w_up, ffn2_w_down, i) for i in range(depth)]

    lambda_init = 0.8 - 0.6 * math.exp(-0.3 * 0)
    x, q, k, v = _pre_diff(x, phases, ffn1[0], _row(mix_norm[0]), diff_w_in[0].astype(BF16))
    diff_params = (_row(diff_lambda_q1[0]), _row(diff_lambda_k1[0]), _row(diff_lambda_q2[0]),
                   _row(diff_lambda_k2[0]), diff_sub_norm[0].reshape(-1, 1).astype(F32))
    o = _attention(q, k, v, diff_params, lambda_init)
    x = _post(x, o, diff_w_out[0].astype(BF16), ffn2[0])

    w_in = jnp.pad(mla_w_in[0], ((0, 0), (0, HEAD_W - MLA_ROPE))).astype(BF16)
    w_q = mla_w_q_up[0].reshape(MLA_Q_RANK, N_HEADS, MLA_NOPE + MLA_ROPE)
    w_q = jnp.pad(w_q, ((0, 0), (0, 0), (0, MLA_QK_W - MLA_NOPE - MLA_ROPE)))
    w_q = w_q.reshape(MLA_Q_RANK, N_HEADS * MLA_QK_W).astype(BF16)
    x, q, k, v = _pre_mla(x, phases, ffn1[1], _row(mix_norm[1]), w_in, _row(mla_q_norm[0]), w_q,
                          _row(mla_kv_norm[0]), mla_w_kv_up[0].astype(BF16))
    o = _attention(q, k, v)
    return _post(x, o, mla_w_out[0].astype(BF16), ffn2[1], _row(final_norm))
```

```python
import functools
import math

import jax
import jax.numpy as jnp
from jax import lax
from jax.experimental import pallas as pl
from jax.experimental.pallas import tpu as pltpu

D_MODEL = 1024
D_FF = 2816
N_HEADS = 8
HEAD_W = 128
ROPE_DIM = 64
ROPE_THETA = 10000.0
NORM_EPS = 1e-6
FFN_RESIDUAL_WEIGHT = 0.5
MLA_NOPE = 128
MLA_ROPE = 64
MLA_Q_RANK = 384
MLA_KV_RANK = 256
MLA_QK_W = 256
DIFF_HEAD_DIM = 64

TOKEN_TILE = 256
ATTN_Q_TILE = 1024
ATTN_K_TILE = ATTN_Q_TILE // 2
VMEM_LIMIT_BYTES = 56 * 1024 * 1024
SOFTMAX_ROWS = 64
MASK_VALUE = -1e30
LOG2_E = math.log2(math.e)

F32 = jnp.float32
BF16 = jnp.bfloat16


def _dot(a, b):
    return jnp.dot(a, b, preferred_element_type=F32)


def _rms(x, gain):
    return x * lax.rsqrt(jnp.mean(x * x, axis=-1, keepdims=True) + NORM_EPS) * gain


def _ffn_residual(x, g_ref, wg_ref, wu_ref, wd_ref):
    h = _rms(x, g_ref[...]).astype(BF16)
    gate = _dot(h, wg_ref[...])
    up = _dot(h, wu_ref[...])
    act = (gate * jax.nn.sigmoid(gate) * up).astype(BF16)
    return x + FFN_RESIDUAL_WEIGHT * _dot(act, wd_ref[...])


def _rope(x, cos_f, sin_a, sin_b):
    return (x * cos_f + pltpu.roll(x, HEAD_W - ROPE_DIM // 2, axis=1) * sin_a
            + pltpu.roll(x, ROPE_DIM // 2, axis=1) * sin_b)


def _rope_tables(cs):
    cos, sin = cs[:, :ROPE_DIM // 2], cs[:, ROPE_DIM // 2:]
    zero = jnp.zeros_like(sin)
    return (jnp.concatenate([cos, cos, cos, cos], axis=1),
            jnp.concatenate([-sin, zero, -sin, zero], axis=1),
            jnp.concatenate([zero, sin, zero, sin], axis=1))


def _pre_diff_kernel(x_ref, cs_ref, fg_ref, wg_ref, wu_ref, wd_ref,
                     mg_ref, win_ref, xo_ref, q_ref, k_ref, v_ref):
    x = _ffn_residual(x_ref[0], fg_ref, wg_ref, wu_ref, wd_ref)
    xo_ref[0] = x
    h = _rms(x, mg_ref[...]).astype(BF16)
    qkv = _dot(h, win_ref[...])
    cos_f, sin_a, sin_b = _rope_tables(cs_ref[0])
    scale = DIFF_HEAD_DIM ** -0.5 * LOG2_E
    for hd in range(N_HEADS):
        lo = hd * HEAD_W
        q = _rope(qkv[:, lo:lo + HEAD_W], cos_f, sin_a, sin_b) * scale
        k = _rope(qkv[:, D_MODEL + lo:D_MODEL + lo + HEAD_W], cos_f, sin_a, sin_b)
        q_ref[0, hd] = q.astype(BF16)
        k_ref[0, hd] = k.astype(BF16)
        v_ref[0, hd] = qkv[:, 2 * D_MODEL + lo:2 * D_MODEL + lo + HEAD_W].astype(BF16)


def _pre_mla_kernel(x_ref, cs_ref, fg_ref, wg_ref, wu_ref, wd_ref,
                    mg_ref, win_ref, qn_ref, wq_ref, kvn_ref, wkv_ref,
                    xo_ref, q_ref, k_ref, v_ref):
    x = _ffn_residual(x_ref[0], fg_ref, wg_ref, wu_ref, wd_ref)
    xo_ref[0] = x
    h = _rms(x, mg_ref[...]).astype(BF16)
    c = _dot(h, win_ref[...])
    cos_f, sin_a, sin_b = _rope_tables(cs_ref[0])
    k_rope = _rope(c[:, MLA_Q_RANK + MLA_KV_RANK:], cos_f, sin_a, sin_b).astype(BF16)
    cq = _rms(c[:, :MLA_Q_RANK], qn_ref[...]).astype(BF16)
    ckv = _rms(c[:, MLA_Q_RANK:MLA_Q_RANK + MLA_KV_RANK], kvn_ref[...]).astype(BF16)
    q = _dot(cq, wq_ref[...])
    kv = _dot(ckv, wkv_ref[...])
    scale = (MLA_NOPE + MLA_ROPE) ** -0.5 * LOG2_E
    for hd in range(N_HEADS):
        lo = hd * MLA_QK_W
        q_ref[0, hd, :, :HEAD_W] = (q[:, lo:lo + HEAD_W] * scale).astype(BF16)
        q_rope = _rope(q[:, lo + HEAD_W:lo + 2 * HEAD_W], cos_f, sin_a, sin_b) * scale
        q_ref[0, hd, :, HEAD_W:] = q_rope.astype(BF16)
        k_ref[0, hd, :, :HEAD_W] = kv[:, lo:lo + HEAD_W].astype(BF16)
        k_ref[0, hd, :, HEAD_W:] = k_rope
        v_ref[0, hd] = kv[:, lo + HEAD_W:lo + 2 * HEAD_W].astype(BF16)


def _post_kernel(*refs, final_norm):
    if final_norm:
        x_ref, o_ref, wo_ref, fg_ref, wg_ref, wu_ref, wd_ref, ng_ref, xo_ref = refs
    else:
        x_ref, o_ref, wo_ref, fg_ref, wg_ref, wu_ref, wd_ref, xo_ref = refs
    o = jnp.concatenate([o_ref[0, hd] for hd in range(N_HEADS)], axis=1)
    x = x_ref[0] + _dot(o, wo_ref[...])
    x = _ffn_residual(x, fg_ref, wg_ref, wu_ref, wd_ref)
    if final_norm:
        x = _rms(x, ng_ref[...])
    xo_ref[0] = x


def _attn_kernel(*refs, tq, tk, diff, lambda_init):
    n_extra = 5 if diff else 0
    q_ref, k_ref, v_ref = refs[:3]
    extra = refs[3:3 + n_extra]
    o_ref = refs[3 + n_extra]
    s_a, s_b, p_a, p_b, al_a, al_b, m_ref, l_ref, acc_ref = refs[4 + n_extra:4 + n_extra + 9]
    assert tq == 2 * tk
    all_rows = (0, tq)
    upper_rows = (tk, tq)
    qi = pl.program_id(2)

    def key_slice(j):
        return pl.ds(pl.multiple_of(j * tk, tk), tk)

    def softmax(s_ref, p_ref, al_ref, rows, diag=None):
        for r0 in range(rows[0], rows[1], SOFTMAX_ROWS):
            blk = slice(r0, r0 + SOFTMAX_ROWS)
            s = s_ref[blk]
            if diag is not None:
                q0 = r0 - diag * tk
                assert q0 + SOFTMAX_ROWS > 0
                if q0 < tk - 1:
                    row = lax.broadcasted_iota(jnp.int32, s.shape, 0) + q0
                    col = lax.broadcasted_iota(jnp.int32, s.shape, 1)
                    s = jnp.where(col <= row, s, MASK_VALUE)
            m_old = m_ref[blk]
            m_blk = jnp.max(s, axis=1, keepdims=True)
            m_new = jnp.maximum(m_old, jnp.broadcast_to(m_blk, m_old.shape))
            alpha = jnp.exp2(m_old - m_new)
            p_lanes = None
            for c in range(0, tk, HEAD_W):
                p = jnp.exp2(s[:, c:c + HEAD_W] - m_new)
                p_ref[blk, c:c + HEAD_W] = p.astype(BF16)
                p_lanes = p if p_lanes is None else p_lanes + p
            l_ref[blk] = alpha * l_ref[blk] + p_lanes
            m_ref[blk] = m_new
            al_ref[blk] = alpha

    def weighted_values(p_ref, al_ref, j, rows):
        blk = slice(*rows)
        acc_ref[blk] = al_ref[blk] * acc_ref[blk] + _dot(p_ref[blk], v_ref[0, 0, key_slice(j), :])

    def attend(q_rows):
        def scores(s_ref, j, rows):
            s_ref[slice(*rows)] = lax.dot_general(
                q_rows(*rows), k_ref[0, 0, key_slice(j), :], (((1,), (1,)), ((), ())),
                preferred_element_type=F32)

        m_ref[...] = jnp.full(m_ref.shape, MASK_VALUE, F32)
        l_ref[...] = jnp.zeros(l_ref.shape, F32)
        acc_ref[...] = jnp.zeros(acc_ref.shape, F32)

        d0 = 2 * qi
        scores(s_a, d0 + 1, upper_rows)
        scores(s_b, d0, all_rows)
        softmax(s_a, p_a, al_a, upper_rows, diag=1)
        weighted_values(p_a, al_a, d0 + 1, upper_rows)
        softmax(s_b, p_b, al_b, all_rows, diag=0)
        scores(s_a, 0, all_rows)

        def pair(i, pending, prefetch=True):
            scores(s_b, 2 * i + 1, all_rows)
            weighted_values(p_b, al_b, pending, all_rows)
            softmax(s_a, p_a, al_a, all_rows)
            if prefetch:
                scores(s_a, 2 * i + 2, all_rows)
            weighted_values(p_a, al_a, 2 * i, all_rows)
            softmax(s_b, p_b, al_b, all_rows)
            return 2 * i + 1

        pending = lax.fori_loop(0, qi - 1, pair, d0)

        @pl.when(qi > 0)
        def _():
            pair(qi - 1, pending, prefetch=False)

        weighted_values(p_b, al_b, jnp.where(qi > 0, 2 * qi - 1, d0), all_rows)
        return acc_ref[...] / jnp.sum(l_ref[...], axis=1, keepdims=True)

    if diff:
        qs_ref, o0_ref = refs[4 + n_extra + 9:]
        lane = lax.broadcasted_iota(jnp.int32, (tq, HEAD_W), 1)

        def half(first):
            keep = (lane < DIFF_HEAD_DIM) if first else (lane >= DIFF_HEAD_DIM)
            qs_ref[...] = jnp.where(keep, q_ref[0, 0], jnp.zeros((tq, HEAD_W), BF16))
            return attend(lambda lo, hi: qs_ref[lo:hi])

        o0_ref[...] = half(True)
        o1 = half(False)
        lq1_ref, lk1_ref, lq2_ref, lk2_ref, sg_ref = extra
        lam = (jnp.exp(jnp.sum(lq1_ref[...] * lk1_ref[...], axis=-1, keepdims=True))
               - jnp.exp(jnp.sum(lq2_ref[...] * lk2_ref[...], axis=-1, keepdims=True))
               + lambda_init)
        o = _rms(o0_ref[...] - lam * o1, sg_ref[...]) * (1.0 - lambda_init)
    else:
        o = attend(lambda lo, hi: q_ref[0, 0, lo:hi])
    o_ref[0, 0] = o.astype(BF16)


def _const_spec(shape):
    zeros = (0,) * len(shape)
    return pl.BlockSpec(shape, lambda *_: zeros, pipeline_mode=pl.Buffered(1))


def _token_specs(b_s, widths):
    del b_s
    return [pl.BlockSpec((1, TOKEN_TILE, w), lambda b, i: (b, i, 0)) for w in widths]


def _head_spec(width):
    return pl.BlockSpec((1, N_HEADS, TOKEN_TILE, width), lambda b, i: (b, 0, i, 0))


def _params():
    return pltpu.CompilerParams(dimension_semantics=("arbitrary", "arbitrary"),
                                vmem_limit_bytes=VMEM_LIMIT_BYTES)


def _ffn_specs():
    return [_const_spec((1, D_MODEL)), _const_spec((D_MODEL, D_FF)), _const_spec((D_MODEL, D_FF)),
            _const_spec((D_FF, D_MODEL))]


def _pre_diff(x, phases, ffn, mix_g, w_in):
    b, s, _ = x.shape
    head = jax.ShapeDtypeStruct((b, N_HEADS, s, HEAD_W), BF16)
    return pl.pallas_call(
        _pre_diff_kernel,
        grid=(b, s // TOKEN_TILE),
        in_specs=_token_specs((b, s), [D_MODEL, ROPE_DIM]) + _ffn_specs()
        + [_const_spec((1, D_MODEL)), _const_spec((D_MODEL, 3 * D_MODEL))],
        out_specs=_token_specs((b, s), [D_MODEL]) + [_head_spec(HEAD_W)] * 3,
        out_shape=[jax.ShapeDtypeStruct(x.shape, F32), head, head, head],
        compiler_params=_params(),
        name="pre_diff",
    )(x, phases, *ffn, mix_g, w_in)


def _pre_mla(x, phases, ffn, mix_g, w_in, q_norm, w_q, kv_norm, w_kv):
    b, s, _ = x.shape
    head = jax.ShapeDtypeStruct((b, N_HEADS, s, HEAD_W), BF16)
    head_qk = jax.ShapeDtypeStruct((b, N_HEADS, s, MLA_QK_W), BF16)
    return pl.pallas_call(
        _pre_mla_kernel,
        grid=(b, s // TOKEN_TILE),
        in_specs=_token_specs((b, s), [D_MODEL, ROPE_DIM]) + _ffn_specs()
        + [_const_spec((1, D_MODEL)), _const_spec(w_in.shape), _const_spec(q_norm.shape),
           _const_spec(w_q.shape), _const_spec(kv_norm.shape), _const_spec(w_kv.shape)],
        out_specs=_token_specs((b, s), [D_MODEL])
        + [_head_spec(MLA_QK_W), _head_spec(MLA_QK_W), _head_spec(HEAD_W)],
        out_shape=[jax.ShapeDtypeStruct(x.shape, F32), head_qk, head_qk, head],
        compiler_params=_params(),
        name="pre_mla",
    )(x, phases, *ffn, mix_g, w_in, q_norm, w_q, kv_norm, w_kv)


def _post(x, o, w_out, ffn, final_gain=None):
    b, s, _ = x.shape
    final_norm = final_gain is not None
    extra = [final_gain] if final_norm else []
    return pl.pallas_call(
        functools.partial(_post_kernel, final_norm=final_norm),
        grid=(b, s // TOKEN_TILE),
        in_specs=_token_specs((b, s), [D_MODEL]) + [_head_spec(HEAD_W), _const_spec(w_out.shape)]
        + _ffn_specs() + [_const_spec((1, D_MODEL))] * len(extra),
        out_specs=_token_specs((b, s), [D_MODEL])[0],
        out_shape=jax.ShapeDtypeStruct(x.shape, F32),
        compiler_params=_params(),
        name="post_final" if final_norm else "post",
    )(x, o, w_out, *ffn, *extra)


def _attention(q, k, v, diff_params=None, lambda_init=0.0):
    b, h, s, dqk = q.shape
    tq, tk = ATTN_Q_TILE, ATTN_K_TILE
    diff = diff_params is not None
    extra = list(diff_params) if diff else []
    return pl.pallas_call(
        functools.partial(_attn_kernel, tq=tq, tk=tk, diff=diff, lambda_init=lambda_init),
        grid=(b, h, s // tq),
        in_specs=[pl.BlockSpec((1, 1, tq, dqk), lambda bi, hi, qi: (bi, hi, qi, 0)),
                  pl.BlockSpec((1, 1, s, dqk), lambda bi, hi, qi: (bi, hi, 0, 0)),
                  pl.BlockSpec((1, 1, s, HEAD_W), lambda bi, hi, qi: (bi, hi, 0, 0))]
        + [pl.BlockSpec(e.shape, lambda bi, hi, qi: (0, 0)) for e in extra],
        out_specs=pl.BlockSpec((1, 1, tq, HEAD_W), lambda bi, hi, qi: (bi, hi, qi, 0)),
        out_shape=jax.ShapeDtypeStruct((b, h, s, HEAD_W), BF16),
        scratch_shapes=[pltpu.VMEM((tq, tk), F32), pltpu.VMEM((tq, tk), F32),
                        pltpu.VMEM((tq, tk), BF16), pltpu.VMEM((tq, tk), BF16)]
        + [pltpu.VMEM((tq, HEAD_W), F32) for _ in range(5)]
        + ([pltpu.VMEM((tq, dqk), BF16), pltpu.VMEM((tq, HEAD_W), F32)] if diff else []),
        compiler_params=pltpu.CompilerParams(
            dimension_semantics=("arbitrary", "arbitrary", "arbitrary"),
            vmem_limit_bytes=VMEM_LIMIT_BYTES),
        name="attn_diff" if diff else "attn_mla",
    )(q, k, v, *extra)


def _rope_phases(positions):
    inv_freq = ROPE_THETA ** (-jnp.arange(0, ROPE_DIM, 2, dtype=F32) / ROPE_DIM)
    ang = positions.astype(F32)[..., None] * inv_freq
    return jnp.concatenate([jnp.cos(ang), jnp.sin(ang)], axis=-1)


def _row(v):
    return v.reshape(1, -1).astype(F32)


def kernel(x, positions, ffn1_norm, ffn1_w_gate, ffn1_w_up, ffn1_w_down, mix_norm, ffn2_norm, ffn2_w_gate, ffn2_w_up, ffn2_w_down, diff_w_in, diff_lambda_q1, diff_lambda_k1, diff_lambda_q2, diff_lambda_k2, diff_sub_norm, diff_w_out, mla_w_in, mla_q_norm, mla_w_q_up, mla_kv_norm, mla_w_kv_up, mla_w_out, final_norm):
    depth = ffn1_norm.shape[0]
    assert depth == 2 and diff_w_in.shape[0] == 1 and mla_w_in.shape[0] == 1
    phases = _rope_phases(positions)

    def ffn(norm, wg, wu, wd, i):
        return (_row(norm[i]), wg[i].astype(BF16), wu[i].astype(BF16), wd[i].astype(BF16))

    ffn1 = [ffn(ffn1_norm, ffn1_w_gate, ffn1_w_up, ffn1_w_down, i) for i in range(depth)]
    ffn2 = [ffn(ffn2_norm, ffn2_w_gate, ffn2_w_up, ffn2_w_down, i) for i in range(depth)]

    lambda_init = 0.8 - 0.6 * math.exp(-0.3 * 0)
    x, q, k, v = _pre_diff(x, phases, ffn1[0], _row(mix_norm[0]), diff_w_in[0].astype(BF16))
    diff_params = (_row(diff_lambda_q1[0]), _row(diff_lambda_k1[0]), _row(diff_lambda_q2[0]),
                   _row(diff_lambda_k2[0]), _row(diff_sub_norm[0]))
    o = _attention(q, k, v, diff_params, lambda_init)
    x = _post(x, o, diff_w_out[0].astype(BF16), ffn2[0])

    w_in = jnp.pad(mla_w_in[0], ((0, 0), (0, HEAD_W - MLA_ROPE))).astype(BF16)
    w_q = mla_w_q_up[0].reshape(MLA_Q_RANK, N_HEADS, MLA_NOPE + MLA_ROPE)
    w_q = jnp.pad(w_q, ((0, 0), (0, 0), (0, MLA_QK_W - MLA_NOPE - MLA_ROPE)))
    w_q = w_q.reshape(MLA_Q_RANK, N_HEADS * MLA_QK_W).astype(BF16)
    x, q, k, v = _pre_mla(x, phases, ffn1[1], _row(mix_norm[1]), w_in, _row(mla_q_norm[0]), w_q,
                          _row(mla_kv_norm[0]), mla_w_kv_up[0].astype(BF16))
    o = _attention(q, k, v)
    return _post(x, o, mla_w_out[0].astype(BF16), ffn2[1], _row(final_norm))
```

```python
import functools
import math

import jax
import jax.numpy as jnp
from jax import lax
from jax.experimental import pallas as pl
from jax.experimental.pallas import tpu as pltpu

D_MODEL = 1024
D_FF = 2816
N_HEADS = 8
HEAD_W = 128
ROPE_DIM = 64
ROPE_THETA = 10000.0
NORM_EPS = 1e-6
FFN_RESIDUAL_WEIGHT = 0.5
MLA_NOPE = 128
MLA_ROPE = 64
MLA_Q_RANK = 384
MLA_KV_RANK = 256
MLA_QK_W = 256
DIFF_HEAD_DIM = 64

TOKEN_TILE = 512
MXU_WIDTH = 256
FF_CHUNKS = ((0, 6 * MXU_WIDTH), (6 * MXU_WIDTH, D_FF))
ATTN_Q_TILE = 1024
ATTN_K_TILE = ATTN_Q_TILE // 2
VMEM_LIMIT_BYTES = 56 * 1024 * 1024
SOFTMAX_ROWS = 64
MASK_VALUE = -1e30
LOG2_E = math.log2(math.e)

F32 = jnp.float32
BF16 = jnp.bfloat16


def _dot(a, b):
    return jnp.dot(a, b, preferred_element_type=F32)


def _rms(x, gain):
    return x * lax.rsqrt(jnp.mean(x * x, axis=-1, keepdims=True) + NORM_EPS) * gain


def _ffn_residual(x, g_ref, wg_ref, wu_ref, wd_ref):
    h = _rms(x, g_ref[...]).astype(BF16)
    out = None
    for lo, hi in FF_CHUNKS:
        gate = _dot(h, wg_ref[:, lo:hi])
        up = _dot(h, wu_ref[:, lo:hi])
        act = (gate * jax.nn.sigmoid(gate) * up).astype(BF16)
        part = _dot(act, wd_ref[lo:hi, :])
        out = part if out is None else out + part
    return x + FFN_RESIDUAL_WEIGHT * out


def _rope(x, cos_f, sin_a, sin_b):
    return (x * cos_f + pltpu.roll(x, HEAD_W - ROPE_DIM // 2, axis=1) * sin_a
            + pltpu.roll(x, ROPE_DIM // 2, axis=1) * sin_b)


def _rope_tables(cs):
    cos, sin = cs[:, :ROPE_DIM // 2], cs[:, ROPE_DIM // 2:]
    zero = jnp.zeros_like(sin)
    return (jnp.concatenate([cos, cos, cos, cos], axis=1),
            jnp.concatenate([-sin, zero, -sin, zero], axis=1),
            jnp.concatenate([zero, sin, zero, sin], axis=1))


def _pre_diff_kernel(x_ref, cs_ref, fg_ref, wg_ref, wu_ref, wd_ref,
                     mg_ref, win_ref, xo_ref, q_ref, k_ref, v_ref):
    x = _ffn_residual(x_ref[0], fg_ref, wg_ref, wu_ref, wd_ref)
    xo_ref[0] = x
    h = _rms(x, mg_ref[...]).astype(BF16)
    qkv = _dot(h, win_ref[...])
    cos_f, sin_a, sin_b = _rope_tables(cs_ref[0])
    scale = DIFF_HEAD_DIM ** -0.5 * LOG2_E
    for hd in range(N_HEADS):
        lo = hd * HEAD_W
        q = _rope(qkv[:, lo:lo + HEAD_W], cos_f, sin_a, sin_b) * scale
        k = _rope(qkv[:, D_MODEL + lo:D_MODEL + lo + HEAD_W], cos_f, sin_a, sin_b)
        q_ref[0, hd] = q.astype(BF16)
        k_ref[0, hd] = k.astype(BF16)
        v_ref[0, hd] = qkv[:, 2 * D_MODEL + lo:2 * D_MODEL + lo + HEAD_W].astype(BF16)


def _pre_mla_kernel(x_ref, cs_ref, fg_ref, wg_ref, wu_ref, wd_ref,
                    mg_ref, win_ref, qn_ref, wq_ref, kvn_ref, wkv_ref,
                    xo_ref, q_ref, k_ref, v_ref):
    x = _ffn_residual(x_ref[0], fg_ref, wg_ref, wu_ref, wd_ref)
    xo_ref[0] = x
    h = _rms(x, mg_ref[...]).astype(BF16)
    c = _dot(h, win_ref[...])
    cos_f, sin_a, sin_b = _rope_tables(cs_ref[0])
    k_rope = _rope(c[:, MLA_Q_RANK + MLA_KV_RANK:], cos_f, sin_a, sin_b).astype(BF16)
    cq = _rms(c[:, :MLA_Q_RANK], qn_ref[...]).astype(BF16)
    ckv = _rms(c[:, MLA_Q_RANK:MLA_Q_RANK + MLA_KV_RANK], kvn_ref[...]).astype(BF16)
    q = _dot(cq, wq_ref[...])
    kv = _dot(ckv, wkv_ref[...])
    scale = (MLA_NOPE + MLA_ROPE) ** -0.5 * LOG2_E
    for hd in range(N_HEADS):
        lo = hd * MLA_QK_W
        q_ref[0, hd, :, :HEAD_W] = (q[:, lo:lo + HEAD_W] * scale).astype(BF16)
        q_rope = _rope(q[:, lo + HEAD_W:lo + 2 * HEAD_W], cos_f, sin_a, sin_b) * scale
        q_ref[0, hd, :, HEAD_W:] = q_rope.astype(BF16)
        k_ref[0, hd, :, :HEAD_W] = kv[:, lo:lo + HEAD_W].astype(BF16)
        k_ref[0, hd, :, HEAD_W:] = k_rope
        v_ref[0, hd] = kv[:, lo + HEAD_W:lo + 2 * HEAD_W].astype(BF16)


def _post_kernel(*refs, final_norm):
    if final_norm:
        x_ref, o_ref, wo_ref, fg_ref, wg_ref, wu_ref, wd_ref, ng_ref, xo_ref = refs
    else:
        x_ref, o_ref, wo_ref, fg_ref, wg_ref, wu_ref, wd_ref, xo_ref = refs
    o = jnp.concatenate([o_ref[0, hd] for hd in range(N_HEADS)], axis=1)
    x = x_ref[0] + _dot(o, wo_ref[...])
    x = _ffn_residual(x, fg_ref, wg_ref, wu_ref, wd_ref)
    if final_norm:
        x = _rms(x, ng_ref[...])
    xo_ref[0] = x


def _attn_kernel(*refs, tq, tk, diff, lambda_init):
    n_extra = 5 if diff else 0
    q_ref, k_ref, v_ref = refs[:3]
    extra = refs[3:3 + n_extra]
    o_ref = refs[3 + n_extra]
    scratch = refs[4 + n_extra:]
    assert tq == 2 * tk
    all_rows = (0, tq)
    upper_rows = (tk, tq)
    qi = pl.program_id(2)
    d0 = 2 * qi

    def key_slice(j):
        return pl.ds(pl.multiple_of(j * tk, tk), tk)

    def make_chain(q_rows, head, bufs):
        s_a, s_b, p_a, p_b, al_a, al_b, m_ref, l_ref, acc_ref = bufs

        def scores(s_ref, j, rows):
            s_ref[slice(*rows)] = lax.dot_general(
                q_rows(*rows), k_ref[0, head, key_slice(j), :], (((1,), (1,)), ((), ())),
                preferred_element_type=F32)

        def softmax(s_ref, p_ref, al_ref, rows, diag=None):
            for r0 in range(rows[0], rows[1], SOFTMAX_ROWS):
                blk = slice(r0, r0 + SOFTMAX_ROWS)
                s = s_ref[blk]
                if diag is not None:
                    q0 = r0 - diag * tk
                    assert q0 + SOFTMAX_ROWS > 0
                    if q0 < tk - 1:
                        row = lax.broadcasted_iota(jnp.int32, s.shape, 0) + q0
                        col = lax.broadcasted_iota(jnp.int32, s.shape, 1)
                        s = jnp.where(col <= row, s, MASK_VALUE)
                m_old = m_ref[blk]
                m_blk = jnp.max(s, axis=1, keepdims=True)
                m_new = jnp.maximum(m_old, jnp.broadcast_to(m_blk, m_old.shape))
                alpha = jnp.exp2(m_old - m_new)
                p_lanes = None
                for c in range(0, tk, HEAD_W):
                    p = jnp.exp2(s[:, c:c + HEAD_W] - m_new)
                    p_ref[blk, c:c + HEAD_W] = p.astype(BF16)
                    p_lanes = p if p_lanes is None else p_lanes + p
                l_ref[blk] = alpha * l_ref[blk] + p_lanes
                m_ref[blk] = m_new
                al_ref[blk] = alpha

        def weighted_values(p_ref, al_ref, j, rows):
            blk = slice(*rows)
            acc_ref[blk] = (al_ref[blk] * acc_ref[blk]
                            + _dot(p_ref[blk], v_ref[0, head, key_slice(j), :]))

        def diagonal():
            m_ref[...] = jnp.full(m_ref.shape, MASK_VALUE, F32)
            l_ref[...] = jnp.zeros(l_ref.shape, F32)
            acc_ref[...] = jnp.zeros(acc_ref.shape, F32)
            scores(s_a, d0 + 1, upper_rows)
            scores(s_b, d0, all_rows)
            softmax(s_a, p_a, al_a, upper_rows, diag=1)
            weighted_values(p_a, al_a, d0 + 1, upper_rows)
            softmax(s_b, p_b, al_b, all_rows, diag=0)
            scores(s_a, 0, all_rows)

        def pair(i, pending, prefetch=True):
            scores(s_b, 2 * i + 1, all_rows)
            weighted_values(p_b, al_b, pending, all_rows)
            softmax(s_a, p_a, al_a, all_rows)
            if prefetch:
                scores(s_a, 2 * i + 2, all_rows)
            weighted_values(p_a, al_a, 2 * i, all_rows)
            softmax(s_b, p_b, al_b, all_rows)
            return 2 * i + 1

        def full_tiles():
            pending = lax.fori_loop(0, qi - 1, pair, d0)

            @pl.when(qi > 0)
            def _():
                pair(qi - 1, pending, prefetch=False)

        def finish():
            weighted_values(p_b, al_b, jnp.where(qi > 0, 2 * qi - 1, d0), all_rows)
            return acc_ref[...] / jnp.sum(l_ref[...], axis=1, keepdims=True)

        return diagonal, full_tiles, finish

    n_bufs = 9
    if diff:
        qs_refs = scratch[2 * n_bufs:]
        lane = lax.broadcasted_iota(jnp.int32, (tq, HEAD_W), 1)
        for qs_ref, keep in zip(qs_refs, (lane < DIFF_HEAD_DIM, lane >= DIFF_HEAD_DIM)):
            qs_ref[...] = jnp.where(keep, q_ref[0, 0], jnp.zeros((tq, HEAD_W), BF16))
        chains = [make_chain(lambda lo, hi, r=qs_ref: r[lo:hi], 0, scratch[c * n_bufs:(c + 1) * n_bufs])
                  for c, qs_ref in enumerate(qs_refs)]
    else:
        chains = [make_chain(lambda lo, hi, h=h: q_ref[0, h, lo:hi], h,
                             scratch[h * n_bufs:(h + 1) * n_bufs]) for h in range(2)]

    for diagonal, _, _ in chains:
        diagonal()
    for _, full_tiles, _ in chains:
        full_tiles()
    outs = [finish() for _, _, finish in chains]

    if diff:
        lq1_ref, lk1_ref, lq2_ref, lk2_ref, sg_ref = extra
        lam = (jnp.exp(jnp.sum(lq1_ref[...] * lk1_ref[...], axis=-1, keepdims=True))
               - jnp.exp(jnp.sum(lq2_ref[...] * lk2_ref[...], axis=-1, keepdims=True))
               + lambda_init)
        o = _rms(outs[0] - lam * outs[1], sg_ref[...]) * (1.0 - lambda_init)
        o_ref[0, 0] = o.astype(BF16)
    else:
        for h, o in enumerate(outs):
            o_ref[0, h] = o.astype(BF16)


def _const_spec(shape):
    zeros = (0,) * len(shape)
    return pl.BlockSpec(shape, lambda *_: zeros, pipeline_mode=pl.Buffered(1))


def _token_specs(b_s, widths):
    del b_s
    return [pl.BlockSpec((1, TOKEN_TILE, w), lambda b, i: (b, i, 0)) for w in widths]


def _head_spec(width):
    return pl.BlockSpec((1, N_HEADS, TOKEN_TILE, width), lambda b, i: (b, 0, i, 0))


def _params():
    return pltpu.CompilerParams(dimension_semantics=("arbitrary", "arbitrary"),
                                vmem_limit_bytes=VMEM_LIMIT_BYTES)


def _ffn_specs():
    return [_const_spec((1, D_MODEL)), _const_spec((D_MODEL, D_FF)), _const_spec((D_MODEL, D_FF)),
            _const_spec((D_FF, D_MODEL))]


def _pre_diff(x, phases, ffn, mix_g, w_in):
    b, s, _ = x.shape
    head = jax.ShapeDtypeStruct((b, N_HEADS, s, HEAD_W), BF16)
    return pl.pallas_call(
        _pre_diff_kernel,
        grid=(b, s // TOKEN_TILE),
        in_specs=_token_specs((b, s), [D_MODEL, ROPE_DIM]) + _ffn_specs()
        + [_const_spec((1, D_MODEL)), _const_spec((D_MODEL, 3 * D_MODEL))],
        out_specs=_token_specs((b, s), [D_MODEL]) + [_head_spec(HEAD_W)] * 3,
        out_shape=[jax.ShapeDtypeStruct(x.shape, F32), head, head, head],
        compiler_params=_params(),
        name="pre_diff",
    )(x, phases, *ffn, mix_g, w_in)


def _pre_mla(x, phases, ffn, mix_g, w_in, q_norm, w_q, kv_norm, w_kv):
    b, s, _ = x.shape
    head = jax.ShapeDtypeStruct((b, N_HEADS, s, HEAD_W), BF16)
    head_qk = jax.ShapeDtypeStruct((b, N_HEADS, s, MLA_QK_W), BF16)
    return pl.pallas_call(
        _pre_mla_kernel,
        grid=(b, s // TOKEN_TILE),
        in_specs=_token_specs((b, s), [D_MODEL, ROPE_DIM]) + _ffn_specs()
        + [_const_spec((1, D_MODEL)), _const_spec(w_in.shape), _const_spec(q_norm.shape),
           _const_spec(w_q.shape), _const_spec(kv_norm.shape), _const_spec(w_kv.shape)],
        out_specs=_token_specs((b, s), [D_MODEL])
        + [_head_spec(MLA_QK_W), _head_spec(MLA_QK_W), _head_spec(HEAD_W)],
        out_shape=[jax.ShapeDtypeStruct(x.shape, F32), head_qk, head_qk, head],
        compiler_params=_params(),
        name="pre_mla",
    )(x, phases, *ffn, mix_g, w_in, q_norm, w_q, kv_norm, w_kv)


def _post(x, o, w_out, ffn, final_gain=None):
    b, s, _ = x.shape
    final_norm = final_gain is not None
    extra = [final_gain] if final_norm else []
    return pl.pallas_call(
        functools.partial(_post_kernel, final_norm=final_norm),
        grid=(b, s // TOKEN_TILE),
        in_specs=_token_specs((b, s), [D_MODEL]) + [_head_spec(HEAD_W), _const_spec(w_out.shape)]
        + _ffn_specs() + [_const_spec((1, D_MODEL))] * len(extra),
        out_specs=_token_specs((b, s), [D_MODEL])[0],
        out_shape=jax.ShapeDtypeStruct(x.shape, F32),
        compiler_params=_params(),
        name="post_final" if final_norm else "post",
    )(x, o, w_out, *ffn, *extra)


def _attention(q, k, v, diff_params=None, lambda_init=0.0):
    b, h, s, dqk = q.shape
    tq, tk = ATTN_Q_TILE, ATTN_K_TILE
    diff = diff_params is not None
    extra = list(diff_params) if diff else []
    heads = 1 if diff else 2

    def chain_scratch():
        return ([pltpu.VMEM((tq, tk), F32), pltpu.VMEM((tq, tk), F32),
                 pltpu.VMEM((tq, tk), BF16), pltpu.VMEM((tq, tk), BF16)]
                + [pltpu.VMEM((tq, HEAD_W), F32) for _ in range(5)])

    return pl.pallas_call(
        functools.partial(_attn_kernel, tq=tq, tk=tk, diff=diff, lambda_init=lambda_init),
        grid=(b, h // heads, s // tq),
        in_specs=[pl.BlockSpec((1, heads, tq, dqk), lambda bi, hi, qi: (bi, hi, qi, 0)),
                  pl.BlockSpec((1, heads, s, dqk), lambda bi, hi, qi: (bi, hi, 0, 0)),
                  pl.BlockSpec((1, heads, s, HEAD_W), lambda bi, hi, qi: (bi, hi, 0, 0))]
        + [pl.BlockSpec(e.shape, lambda bi, hi, qi: (0, 0)) for e in extra],
        out_specs=pl.BlockSpec((1, heads, tq, HEAD_W), lambda bi, hi, qi: (bi, hi, qi, 0)),
        out_shape=jax.ShapeDtypeStruct((b, h, s, HEAD_W), BF16),
        scratch_shapes=chain_scratch() + chain_scratch()
        + ([pltpu.VMEM((tq, dqk), BF16), pltpu.VMEM((tq, dqk), BF16)] if diff else []),
        compiler_params=pltpu.CompilerParams(
            dimension_semantics=("arbitrary", "arbitrary", "arbitrary"),
            vmem_limit_bytes=VMEM_LIMIT_BYTES),
        name="attn_diff" if diff else "attn_mla",
    )(q, k, v, *extra)


def _rope_phases(positions):
    inv_freq = ROPE_THETA ** (-jnp.arange(0, ROPE_DIM, 2, dtype=F32) / ROPE_DIM)
    ang = positions.astype(F32)[..., None] * inv_freq
    return jnp.concatenate([jnp.cos(ang), jnp.sin(ang)], axis=-1)


def _row(v):
    return v.reshape(1, -1).astype(F32)


def kernel(x, positions, ffn1_norm, ffn1_w_gate, ffn1_w_up, ffn1_w_down, mix_norm, ffn2_norm, ffn2_w_gate, ffn2_w_up, ffn2_w_down, diff_w_in, diff_lambda_q1, diff_lambda_k1, diff_lambda_q2, diff_lambda_k2, diff_sub_norm, diff_w_out, mla_w_in, mla_q_norm, mla_w_q_up, mla_kv_norm, mla_w_kv_up, mla_w_out, final_norm):
    depth = ffn1_norm.shape[0]
    assert depth == 2 and diff_w_in.shape[0] == 1 and mla_w_in.shape[0] == 1
    phases = _rope_phases(positions)

    def ffn(norm, wg, wu, wd, i):
        return (_row(norm[i]), wg[i].astype(BF16), wu[i].astype(BF16), wd[i].astype(BF16))

    ffn1 = [ffn(ffn1_norm, ffn1_w_gate, ffn1_w_up, ffn1_w_down, i) for i in range(depth)]
    ffn2 = [ffn(ffn2_norm, ffn2_w_gate, ffn2_w_up, ffn2_w_down, i) for i in range(depth)]

    lambda_init = 0.8 - 0.6 * math.exp(-0.3 * 0)
    x, q, k, v = _pre_diff(x, phases, ffn1[0], _row(mix_norm[0]), diff_w_in[0].astype(BF16))
    diff_params = (_row(diff_lambda_q1[0]), _row(diff_lambda_k1[0]), _row(diff_lambda_q2[0]),
                   _row(diff_lambda_k2[0]), _row(diff_sub_norm[0]))
    o = _attention(q, k, v, diff_params, lambda_init)
    x = _post(x, o, diff_w_out[0].astype(BF16), ffn2[0])

    w_in = jnp.pad(mla_w_in[0], ((0, 0), (0, HEAD_W - MLA_ROPE))).astype(BF16)
    w_q = mla_w_q_up[0].reshape(MLA_Q_RANK, N_HEADS, MLA_NOPE + MLA_ROPE)
    w_q = jnp.pad(w_q, ((0, 0), (0, 0), (0, MLA_QK_W - MLA_NOPE - MLA_ROPE)))
    w_q = w_q.reshape(MLA_Q_RANK, N_HEADS * MLA_QK_W).astype(BF16)
    x, q, k, v = _pre_mla(x, phases, ffn1[1], _row(mix_norm[1]), w_in, _row(mla_q_norm[0]), w_q,
                          _row(mla_kv_norm[0]), mla_w_kv_up[0].astype(BF16))
    o = _attention(q, k, v)
    return _post(x, o, mla_w_out[0].astype(BF16), ffn2[1], _row(final_norm))
```

```python
import functools
import math

import jax
import jax.numpy as jnp
from jax import lax
from jax.experimental import pallas as pl
from jax.experimental.pallas import tpu as pltpu

D_MODEL = 1024
D_FF = 2816
N_HEADS = 8
HEAD_W = 128
ROPE_DIM = 64
ROPE_THETA = 10000.0
NORM_EPS = 1e-6
FFN_RESIDUAL_WEIGHT = 0.5
MLA_NOPE = 128
MLA_ROPE = 64
MLA_Q_RANK = 384
MLA_KV_RANK = 256
MLA_QK_W = 256
DIFF_HEAD_DIM = 64

TOKEN_TILE = 512
MXU_WIDTH = 256
FF_CHUNKS = ((0, 6 * MXU_WIDTH), (6 * MXU_WIDTH, D_FF))
ATTN_Q_TILE = 1024
ATTN_K_TILE = ATTN_Q_TILE // 2
VMEM_LIMIT_BYTES = 56 * 1024 * 1024
SOFTMAX_ROWS = 64
MASK_VALUE = -1e30
LOG2_E = math.log2(math.e)

F32 = jnp.float32
BF16 = jnp.bfloat16


def _dot(a, b):
    return jnp.dot(a, b, preferred_element_type=F32)


def _rms(x, gain):
    return x * lax.rsqrt(jnp.mean(x * x, axis=-1, keepdims=True) + NORM_EPS) * gain


def _ffn_residual(x, g_ref, wg_ref, wu_ref, wd_ref):
    h = _rms(x, g_ref[...]).astype(BF16)
    out = None
    for lo, hi in FF_CHUNKS:
        gate = _dot(h, wg_ref[:, lo:hi])
        up = _dot(h, wu_ref[:, lo:hi])
        act = (gate * jax.nn.sigmoid(gate) * up).astype(BF16)
        part = _dot(act, wd_ref[lo:hi, :])
        out = part if out is None else out + part
    return x + FFN_RESIDUAL_WEIGHT * out


def _rope(x, cos_f, sin_a, sin_b):
    return (x * cos_f + pltpu.roll(x, HEAD_W - ROPE_DIM // 2, axis=1) * sin_a
            + pltpu.roll(x, ROPE_DIM // 2, axis=1) * sin_b)


def _rope_tables(cs):
    cos, sin = cs[:, :ROPE_DIM // 2], cs[:, ROPE_DIM // 2:]
    zero = jnp.zeros_like(sin)
    return (jnp.concatenate([cos, cos, cos, cos], axis=1),
            jnp.concatenate([-sin, zero, -sin, zero], axis=1),
            jnp.concatenate([zero, sin, zero, sin], axis=1))


def _pre_diff_kernel(x_ref, cs_ref, fg_ref, wg_ref, wu_ref, wd_ref,
                     mg_ref, win_ref, xo_ref, q_ref, k_ref, v_ref):
    x = _ffn_residual(x_ref[0], fg_ref, wg_ref, wu_ref, wd_ref)
    xo_ref[0] = x
    h = _rms(x, mg_ref[...]).astype(BF16)
    qkv = _dot(h, win_ref[...])
    cos_f, sin_a, sin_b = _rope_tables(cs_ref[0])
    scale = DIFF_HEAD_DIM ** -0.5 * LOG2_E
    for hd in range(N_HEADS):
        lo = hd * HEAD_W
        q = _rope(qkv[:, lo:lo + HEAD_W], cos_f, sin_a, sin_b) * scale
        k = _rope(qkv[:, D_MODEL + lo:D_MODEL + lo + HEAD_W], cos_f, sin_a, sin_b)
        q_ref[0, hd] = q.astype(BF16)
        k_ref[0, hd] = k.astype(BF16)
        v_ref[0, hd] = qkv[:, 2 * D_MODEL + lo:2 * D_MODEL + lo + HEAD_W].astype(BF16)


def _pre_mla_kernel(x_ref, cs_ref, fg_ref, wg_ref, wu_ref, wd_ref,
                    mg_ref, win_ref, qn_ref, wq_ref, kvn_ref, wkv_ref,
                    xo_ref, q_ref, k_ref, v_ref):
    x = _ffn_residual(x_ref[0], fg_ref, wg_ref, wu_ref, wd_ref)
    xo_ref[0] = x
    h = _rms(x, mg_ref[...]).astype(BF16)
    c = _dot(h, win_ref[...])
    cos_f, sin_a, sin_b = _rope_tables(cs_ref[0])
    k_rope = _rope(c[:, MLA_Q_RANK + MLA_KV_RANK:], cos_f, sin_a, sin_b).astype(BF16)
    cq = _rms(c[:, :MLA_Q_RANK], qn_ref[...]).astype(BF16)
    ckv = _rms(c[:, MLA_Q_RANK:MLA_Q_RANK + MLA_KV_RANK], kvn_ref[...]).astype(BF16)
    q = _dot(cq, wq_ref[...])
    kv = _dot(ckv, wkv_ref[...])
    scale = (MLA_NOPE + MLA_ROPE) ** -0.5 * LOG2_E
    for hd in range(N_HEADS):
        lo = hd * MLA_QK_W
        q_ref[0, hd, :, :HEAD_W] = (q[:, lo:lo + HEAD_W] * scale).astype(BF16)
        q_rope = _rope(q[:, lo + HEAD_W:lo + 2 * HEAD_W], cos_f, sin_a, sin_b) * scale
        q_ref[0, hd, :, HEAD_W:] = q_rope.astype(BF16)
        k_ref[0, hd, :, :HEAD_W] = kv[:, lo:lo + HEAD_W].astype(BF16)
        k_ref[0, hd, :, HEAD_W:] = k_rope
        v_ref[0, hd] = kv[:, lo + HEAD_W:lo + 2 * HEAD_W].astype(BF16)


def _post_kernel(*refs, final_norm):
    if final_norm:
        x_ref, o_ref, wo_ref, fg_ref, wg_ref, wu_ref, wd_ref, ng_ref, xo_ref = refs
    else:
        x_ref, o_ref, wo_ref, fg_ref, wg_ref, wu_ref, wd_ref, xo_ref = refs
    o = jnp.concatenate([o_ref[0, hd] for hd in range(N_HEADS)], axis=1)
    x = x_ref[0] + _dot(o, wo_ref[...])
    x = _ffn_residual(x, fg_ref, wg_ref, wu_ref, wd_ref)
    if final_norm:
        x = _rms(x, ng_ref[...])
    xo_ref[0] = x


def _attn_kernel(*refs, tq, tk, diff, lambda_init):
    n_extra = 5 if diff else 0
    q_ref, k_ref, v_ref = refs[:3]
    extra = refs[3:3 + n_extra]
    o_ref = refs[3 + n_extra]
    scratch = refs[4 + n_extra:]
    assert tq == 2 * tk
    all_rows = (0, tq)
    upper_rows = (tk, tq)
    qi = pl.program_id(2)
    d0 = 2 * qi

    def key_slice(j):
        return pl.ds(pl.multiple_of(j * tk, tk), tk)

    def make_chain(q_rows, head, bufs):
        s_a, s_b, p_a, p_b, al_a, al_b, m_ref, l_ref, acc_ref = bufs

        def scores(s_ref, j, rows):
            s_ref[slice(*rows)] = lax.dot_general(
                q_rows(*rows), k_ref[0, head, key_slice(j), :], (((1,), (1,)), ((), ())),
                preferred_element_type=F32)

        def softmax(s_ref, p_ref, al_ref, rows, diag=None):
            for r0 in range(rows[0], rows[1], SOFTMAX_ROWS):
                blk = slice(r0, r0 + SOFTMAX_ROWS)
                s = s_ref[blk]
                if diag is not None:
                    q0 = r0 - diag * tk
                    assert q0 + SOFTMAX_ROWS > 0
                    if q0 < tk - 1:
                        row = lax.broadcasted_iota(jnp.int32, s.shape, 0) + q0
                        col = lax.broadcasted_iota(jnp.int32, s.shape, 1)
                        s = jnp.where(col <= row, s, MASK_VALUE)
                m_old = m_ref[blk]
                m_blk = jnp.max(s, axis=1, keepdims=True)
                m_new = jnp.maximum(m_old, jnp.broadcast_to(m_blk, m_old.shape))
                alpha = jnp.exp2(m_old - m_new)
                p_lanes = None
                for c in range(0, tk, HEAD_W):
                    p = jnp.exp2(s[:, c:c + HEAD_W] - m_new)
                    p_ref[blk, c:c + HEAD_W] = p.astype(BF16)
                    p_lanes = p if p_lanes is None else p_lanes + p
                l_ref[blk] = alpha * l_ref[blk] + p_lanes
                m_ref[blk] = m_new
                al_ref[blk] = alpha

        def weighted_values(p_ref, al_ref, j, rows):
            blk = slice(*rows)
            acc_ref[blk] = (al_ref[blk] * acc_ref[blk]
                            + _dot(p_ref[blk], v_ref[0, head, key_slice(j), :]))

        def diagonal():
            m_ref[...] = jnp.full(m_ref.shape, MASK_VALUE, F32)
            l_ref[...] = jnp.zeros(l_ref.shape, F32)
            acc_ref[...] = jnp.zeros(acc_ref.shape, F32)
            scores(s_a, d0 + 1, upper_rows)
            scores(s_b, d0, all_rows)
            softmax(s_a, p_a, al_a, upper_rows, diag=1)
            weighted_values(p_a, al_a, d0 + 1, upper_rows)
            softmax(s_b, p_b, al_b, all_rows, diag=0)
            scores(s_a, 0, all_rows)

        def pair(i, pending, prefetch=True):
            scores(s_b, 2 * i + 1, all_rows)
            weighted_values(p_b, al_b, pending, all_rows)
            softmax(s_a, p_a, al_a, all_rows)
            if prefetch:
                scores(s_a, 2 * i + 2, all_rows)
            weighted_values(p_a, al_a, 2 * i, all_rows)
            softmax(s_b, p_b, al_b, all_rows)
            return 2 * i + 1

        def full_tiles():
            pending = lax.fori_loop(0, qi - 1, pair, d0)

            @pl.when(qi > 0)
            def _():
                pair(qi - 1, pending, prefetch=False)

        def finish():
            weighted_values(p_b, al_b, jnp.where(qi > 0, 2 * qi - 1, d0), all_rows)
            return acc_ref[...] / jnp.sum(l_ref[...], axis=1, keepdims=True)

        return diagonal, full_tiles, finish

    n_bufs = 9
    if diff:
        qs_refs = scratch[2 * n_bufs:]
        lane = lax.broadcasted_iota(jnp.int32, (tq, HEAD_W), 1)
        for qs_ref, keep in zip(qs_refs, (lane < DIFF_HEAD_DIM, lane >= DIFF_HEAD_DIM)):
            qs_ref[...] = jnp.where(keep, q_ref[0, 0], jnp.zeros((tq, HEAD_W), BF16))
        chains = [make_chain(lambda lo, hi, r=qs_ref: r[lo:hi], 0, scratch[c * n_bufs:(c + 1) * n_bufs])
                  for c, qs_ref in enumerate(qs_refs)]
    else:
        chains = [make_chain(lambda lo, hi, h=h: q_ref[0, h, lo:hi], h,
                             scratch[h * n_bufs:(h + 1) * n_bufs]) for h in range(2)]

    for diagonal, _, _ in chains:
        diagonal()
    for _, full_tiles, _ in chains:
        full_tiles()
    outs = [finish() for _, _, finish in chains]

    if diff:
        lq1_ref, lk1_ref, lq2_ref, lk2_ref, sg_ref = extra
        lam = (jnp.exp(jnp.sum(lq1_ref[...] * lk1_ref[...], axis=-1, keepdims=True))
               - jnp.exp(jnp.sum(lq2_ref[...] * lk2_ref[...], axis=-1, keepdims=True))
               + lambda_init)
        o = _rms(outs[0] - lam * outs[1], sg_ref[...]) * (1.0 - lambda_init)
        o_ref[0, 0] = o.astype(BF16)
    else:
        for h, o in enumerate(outs):
            o_ref[0, h] = o.astype(BF16)


def _const_spec(shape):
    zeros = (0,) * len(shape)
    return pl.BlockSpec(shape, lambda *_: zeros, pipeline_mode=pl.Buffered(1))


def _token_specs(b_s, widths):
    del b_s
    return [pl.BlockSpec((1, TOKEN_TILE, w), lambda b, i: (b, i, 0)) for w in widths]


def _head_spec(width):
    return pl.BlockSpec((1, N_HEADS, TOKEN_TILE, width), lambda b, i: (b, 0, i, 0))


def _params():
    return pltpu.CompilerParams(dimension_semantics=("arbitrary", "arbitrary"),
                                vmem_limit_bytes=VMEM_LIMIT_BYTES)


def _layer_spec(shape, layer):
    return pl.BlockSpec((None,) + shape, lambda *_: (layer, 0, 0), pipeline_mode=pl.Buffered(1))


def _ffn_specs(layer):
    return [_const_spec((1, D_MODEL)), _layer_spec((D_MODEL, D_FF), layer),
            _layer_spec((D_MODEL, D_FF), layer), _layer_spec((D_FF, D_MODEL), layer)]


def _pre_diff(x, phases, ffn, mix_g, w_in):
    b, s, _ = x.shape
    head = jax.ShapeDtypeStruct((b, N_HEADS, s, HEAD_W), BF16)
    return pl.pallas_call(
        _pre_diff_kernel,
        grid=(b, s // TOKEN_TILE),
        in_specs=_token_specs((b, s), [D_MODEL, ROPE_DIM]) + _ffn_specs(ffn[0])
        + [_const_spec((1, D_MODEL)), _const_spec((D_MODEL, 3 * D_MODEL))],
        out_specs=_token_specs((b, s), [D_MODEL]) + [_head_spec(HEAD_W)] * 3,
        out_shape=[jax.ShapeDtypeStruct(x.shape, F32), head, head, head],
        compiler_params=_params(),
        name="pre_diff",
    )(x, phases, *ffn[1:], mix_g, w_in)


def _pre_mla(x, phases, ffn, mix_g, w_in, q_norm, w_q, kv_norm, w_kv):
    b, s, _ = x.shape
    head = jax.ShapeDtypeStruct((b, N_HEADS, s, HEAD_W), BF16)
    head_qk = jax.ShapeDtypeStruct((b, N_HEADS, s, MLA_QK_W), BF16)
    return pl.pallas_call(
        _pre_mla_kernel,
        grid=(b, s // TOKEN_TILE),
        in_specs=_token_specs((b, s), [D_MODEL, ROPE_DIM]) + _ffn_specs(ffn[0])
        + [_const_spec((1, D_MODEL)), _const_spec(w_in.shape), _const_spec(q_norm.shape),
           _const_spec(w_q.shape), _const_spec(kv_norm.shape), _const_spec(w_kv.shape)],
        out_specs=_token_specs((b, s), [D_MODEL])
        + [_head_spec(MLA_QK_W), _head_spec(MLA_QK_W), _head_spec(HEAD_W)],
        out_shape=[jax.ShapeDtypeStruct(x.shape, F32), head_qk, head_qk, head],
        compiler_params=_params(),
        name="pre_mla",
    )(x, phases, *ffn[1:], mix_g, w_in, q_norm, w_q, kv_norm, w_kv)


def _post(x, o, w_out, ffn, final_gain=None):
    b, s, _ = x.shape
    final_norm = final_gain is not None
    extra = [final_gain] if final_norm else []
    return pl.pallas_call(
        functools.partial(_post_kernel, final_norm=final_norm),
        grid=(b, s // TOKEN_TILE),
        in_specs=_token_specs((b, s), [D_MODEL]) + [_head_spec(HEAD_W), _const_spec(w_out.shape)]
        + _ffn_specs(ffn[0]) + [_const_spec((1, D_MODEL))] * len(extra),
        out_specs=_token_specs((b, s), [D_MODEL])[0],
        out_shape=jax.ShapeDtypeStruct(x.shape, F32),
        compiler_params=_params(),
        name="post_final" if final_norm else "post",
    )(x, o, w_out, *ffn[1:], *extra)


def _attention(q, k, v, diff_params=None, lambda_init=0.0):
    b, h, s, dqk = q.shape
    tq, tk = ATTN_Q_TILE, ATTN_K_TILE
    diff = diff_params is not None
    extra = list(diff_params) if diff else []
    heads = 1 if diff else 2

    def chain_scratch():
        return ([pltpu.VMEM((tq, tk), F32), pltpu.VMEM((tq, tk), F32),
                 pltpu.VMEM((tq, tk), BF16), pltpu.VMEM((tq, tk), BF16)]
                + [pltpu.VMEM((tq, HEAD_W), F32) for _ in range(5)])

    return pl.pallas_call(
        functools.partial(_attn_kernel, tq=tq, tk=tk, diff=diff, lambda_init=lambda_init),
        grid=(b, h // heads, s // tq),
        in_specs=[pl.BlockSpec((1, heads, tq, dqk), lambda bi, hi, qi: (bi, hi, qi, 0)),
                  pl.BlockSpec((1, heads, s, dqk), lambda bi, hi, qi: (bi, hi, 0, 0)),
                  pl.BlockSpec((1, heads, s, HEAD_W), lambda bi, hi, qi: (bi, hi, 0, 0))]
        + [pl.BlockSpec(e.shape, lambda bi, hi, qi: (0, 0)) for e in extra],
        out_specs=pl.BlockSpec((1, heads, tq, HEAD_W), lambda bi, hi, qi: (bi, hi, qi, 0)),
        out_shape=jax.ShapeDtypeStruct((b, h, s, HEAD_W), BF16),
        scratch_shapes=chain_scratch() + chain_scratch()
        + ([pltpu.VMEM((tq, dqk), BF16), pltpu.VMEM((tq, dqk), BF16)] if diff else []),
        compiler_params=pltpu.CompilerParams(
            dimension_semantics=("arbitrary", "arbitrary", "arbitrary"),
            vmem_limit_bytes=VMEM_LIMIT_BYTES),
        name="attn_diff" if diff else "attn_mla",
    )(q, k, v, *extra)


def _rope_phases(positions):
    b, s = positions.shape
    inv_freq = ROPE_THETA ** (-jnp.arange(0, ROPE_DIM, 2, dtype=F32) / ROPE_DIM)
    freq = jnp.tile(inv_freq, 4)
    ang = positions.astype(F32).reshape(b, s // 2, 2, 1) * freq.reshape(2, ROPE_DIM)
    is_cos = (jnp.arange(ROPE_DIM) < ROPE_DIM // 2)
    return jnp.where(is_cos, jnp.cos(ang), jnp.sin(ang)).reshape(b, s, ROPE_DIM)


def _row(v):
    return v.reshape(1, -1).astype(F32)


def kernel(x, positions, ffn1_norm, ffn1_w_gate, ffn1_w_up, ffn1_w_down, mix_norm, ffn2_norm, ffn2_w_gate, ffn2_w_up, ffn2_w_down, diff_w_in, diff_lambda_q1, diff_lambda_k1, diff_lambda_q2, diff_lambda_k2, diff_sub_norm, diff_w_out, mla_w_in, mla_q_norm, mla_w_q_up, mla_kv_norm, mla_w_kv_up, mla_w_out, final_norm):
    depth = ffn1_norm.shape[0]
    assert depth == 2 and diff_w_in.shape[0] == 1 and mla_w_in.shape[0] == 1
    phases = _rope_phases(positions)

    def ffn(norm, wg, wu, wd):
        wg, wu, wd = wg.astype(BF16), wu.astype(BF16), wd.astype(BF16)
        return [(i, _row(norm[i]), wg, wu, wd) for i in range(depth)]

    ffn1 = ffn(ffn1_norm, ffn1_w_gate, ffn1_w_up, ffn1_w_down)
    ffn2 = ffn(ffn2_norm, ffn2_w_gate, ffn2_w_up, ffn2_w_down)

    lambda_init = 0.8 - 0.6 * math.exp(-0.3 * 0)
    x, q, k, v = _pre_diff(x, phases, ffn1[0], _row(mix_norm[0]), diff_w_in[0].astype(BF16))
    diff_params = (_row(diff_lambda_q1[0]), _row(diff_lambda_k1[0]), _row(diff_lambda_q2[0]),
                   _row(diff_lambda_k2[0]), _row(diff_sub_norm[0]))
    o = _attention(q, k, v, diff_params, lambda_init)
    x = _post(x, o, diff_w_out[0].astype(BF16), ffn2[0])

    w_in = jnp.pad(mla_w_in[0], ((0, 0), (0, HEAD_W - MLA_ROPE))).astype(BF16)
    w_q = mla_w_q_up[0].reshape(MLA_Q_RANK, N_HEADS, MLA_NOPE + MLA_ROPE)
    w_q = jnp.pad(w_q, ((0, 0), (0, 0), (0, MLA_QK_W - MLA_NOPE - MLA_ROPE)))
    w_q = w_q.reshape(MLA_Q_RANK, N_HEADS * MLA_QK_W).astype(BF16)
    x, q, k, v = _pre_mla(x, phases, ffn1[1], _row(mix_norm[1]), w_in, _row(mla_q_norm[0]), w_q,
                          _row(mla_kv_norm[0]), mla_w_kv_up[0].astype(BF16))
    o = _attention(q, k, v)
    return _post(x, o, mla_w_out[0].astype(BF16), ffn2[1], _row(final_norm))
```

```python
import functools
import math

import jax
import jax.numpy as jnp
from jax import lax
from jax.experimental import pallas as pl
from jax.experimental.pallas import tpu as pltpu

D_MODEL = 1024
D_FF = 2816
N_HEADS = 8
HEAD_W = 128
ROPE_DIM = 64
ROPE_THETA = 10000.0
NORM_EPS = 1e-6
FFN_RESIDUAL_WEIGHT = 0.5
MLA_NOPE = 128
MLA_ROPE = 64
MLA_Q_RANK = 384
MLA_KV_RANK = 256
MLA_QK_W = 256
DIFF_HEAD_DIM = 64

TOKEN_TILE = 512
MXU_WIDTH = 256
FF_CHUNKS = ((0, 6 * MXU_WIDTH), (6 * MXU_WIDTH, D_FF))
ATTN_Q_TILE = 1024
ATTN_K_TILE = ATTN_Q_TILE // 2
VMEM_LIMIT_BYTES = 56 * 1024 * 1024
SOFTMAX_ROWS = 64
MASK_VALUE = -1e30
LOG2_E = math.log2(math.e)

F32 = jnp.float32
BF16 = jnp.bfloat16


def _dot(a, b):
    return jnp.dot(a, b, preferred_element_type=F32)


def _rms(x, gain):
    return x * lax.rsqrt(jnp.mean(x * x, axis=-1, keepdims=True) + NORM_EPS) * gain


def _ffn_residual(x, g_ref, wg_ref, wu_ref, wd_ref):
    h = _rms(x, g_ref[...]).astype(BF16)
    out = None
    for lo, hi in FF_CHUNKS:
        gate = _dot(h, wg_ref[:, lo:hi])
        up = _dot(h, wu_ref[:, lo:hi])
        act = (gate * jax.nn.sigmoid(gate) * up).astype(BF16)
        part = _dot(act, wd_ref[lo:hi, :])
        out = part if out is None else out + part
    return x + FFN_RESIDUAL_WEIGHT * out


def _rope(x, cos_f, sin_a, sin_b):
    return (x * cos_f + pltpu.roll(x, HEAD_W - ROPE_DIM // 2, axis=1) * sin_a
            + pltpu.roll(x, ROPE_DIM // 2, axis=1) * sin_b)


def _rope_tables(cos, sin):
    zero = jnp.zeros_like(sin)
    return (jnp.concatenate([cos, cos, cos, cos], axis=1),
            jnp.concatenate([-sin, zero, -sin, zero], axis=1),
            jnp.concatenate([zero, sin, zero, sin], axis=1))


def _pre_diff_kernel(x_ref, cos_ref, sin_ref, fg_ref, wg_ref, wu_ref, wd_ref,
                     mg_ref, win_ref, xo_ref, q_ref, k_ref, v_ref):
    x = _ffn_residual(x_ref[0], fg_ref, wg_ref, wu_ref, wd_ref)
    xo_ref[0] = x
    h = _rms(x, mg_ref[...]).astype(BF16)
    qkv = _dot(h, win_ref[...])
    cos_f, sin_a, sin_b = _rope_tables(cos_ref[0], sin_ref[0])
    scale = DIFF_HEAD_DIM ** -0.5 * LOG2_E
    for hd in range(N_HEADS):
        lo = hd * HEAD_W
        q = _rope(qkv[:, lo:lo + HEAD_W], cos_f, sin_a, sin_b) * scale
        k = _rope(qkv[:, D_MODEL + lo:D_MODEL + lo + HEAD_W], cos_f, sin_a, sin_b)
        q_ref[0, hd] = q.astype(BF16)
        k_ref[0, hd] = k.astype(BF16)
        v_ref[0, hd] = qkv[:, 2 * D_MODEL + lo:2 * D_MODEL + lo + HEAD_W].astype(BF16)


def _pre_mla_kernel(x_ref, cos_ref, sin_ref, fg_ref, wg_ref, wu_ref, wd_ref,
                    mg_ref, win_ref, qn_ref, wq_ref, kvn_ref, wkv_ref,
                    xo_ref, q_ref, k_ref, v_ref):
    x = _ffn_residual(x_ref[0], fg_ref, wg_ref, wu_ref, wd_ref)
    xo_ref[0] = x
    h = _rms(x, mg_ref[...]).astype(BF16)
    c = _dot(h, win_ref[...])
    cos_f, sin_a, sin_b = _rope_tables(cos_ref[0], sin_ref[0])
    k_rope = _rope(c[:, MLA_Q_RANK + MLA_KV_RANK:], cos_f, sin_a, sin_b).astype(BF16)
    cq = _rms(c[:, :MLA_Q_RANK], qn_ref[...]).astype(BF16)
    ckv = _rms(c[:, MLA_Q_RANK:MLA_Q_RANK + MLA_KV_RANK], kvn_ref[...]).astype(BF16)
    q = _dot(cq, wq_ref[...])
    kv = _dot(ckv, wkv_ref[...])
    scale = (MLA_NOPE + MLA_ROPE) ** -0.5 * LOG2_E
    for hd in range(N_HEADS):
        lo = hd * MLA_QK_W
        q_ref[0, hd, :, :HEAD_W] = (q[:, lo:lo + HEAD_W] * scale).astype(BF16)
        q_rope = _rope(q[:, lo + HEAD_W:lo + 2 * HEAD_W], cos_f, sin_a, sin_b) * scale
        q_ref[0, hd, :, HEAD_W:] = q_rope.astype(BF16)
        k_ref[0, hd, :, :HEAD_W] = kv[:, lo:lo + HEAD_W].astype(BF16)
        k_ref[0, hd, :, HEAD_W:] = k_rope
        v_ref[0, hd] = kv[:, lo + HEAD_W:lo + 2 * HEAD_W].astype(BF16)


def _post_kernel(*refs, final_norm):
    if final_norm:
        x_ref, o_ref, wo_ref, fg_ref, wg_ref, wu_ref, wd_ref, ng_ref, xo_ref = refs
    else:
        x_ref, o_ref, wo_ref, fg_ref, wg_ref, wu_ref, wd_ref, xo_ref = refs
    o = jnp.concatenate([o_ref[0, hd] for hd in range(N_HEADS)], axis=1)
    x = x_ref[0] + _dot(o, wo_ref[...])
    x = _ffn_residual(x, fg_ref, wg_ref, wu_ref, wd_ref)
    if final_norm:
        x = _rms(x, ng_ref[...])
    xo_ref[0] = x


def _attn_kernel(*refs, tq, tk, diff, lambda_init):
    n_extra = 5 if diff else 0
    q_ref, k_ref, v_ref = refs[:3]
    extra = refs[3:3 + n_extra]
    o_ref = refs[3 + n_extra]
    scratch = refs[4 + n_extra:]
    assert tq == 2 * tk
    all_rows = (0, tq)
    upper_rows = (tk, tq)
    qi = pl.program_id(2)
    d0 = 2 * qi

    def key_slice(j):
        return pl.ds(pl.multiple_of(j * tk, tk), tk)

    def make_chain(q_rows, head, bufs):
        s_a, s_b, p_a, p_b, al_a, al_b, m_ref, l_ref, acc_ref = bufs

        def scores(s_ref, j, rows):
            s_ref[slice(*rows)] = lax.dot_general(
                q_rows(*rows), k_ref[0, head, key_slice(j), :], (((1,), (1,)), ((), ())),
                preferred_element_type=F32)

        def softmax(s_ref, p_ref, al_ref, rows, diag=None):
            for r0 in range(rows[0], rows[1], SOFTMAX_ROWS):
                blk = slice(r0, r0 + SOFTMAX_ROWS)
                s = s_ref[blk]
                if diag is not None:
                    q0 = r0 - diag * tk
                    assert q0 + SOFTMAX_ROWS > 0
                    if q0 < tk - 1:
                        row = lax.broadcasted_iota(jnp.int32, s.shape, 0) + q0
                        col = lax.broadcasted_iota(jnp.int32, s.shape, 1)
                        s = jnp.where(col <= row, s, MASK_VALUE)
                m_old = m_ref[blk]
                m_blk = jnp.max(s, axis=1, keepdims=True)
                m_new = jnp.maximum(m_old, jnp.broadcast_to(m_blk, m_old.shape))
                alpha = jnp.exp2(m_old - m_new)
                p_lanes = None
                for c in range(0, tk, HEAD_W):
                    p = jnp.exp2(s[:, c:c + HEAD_W] - m_new)
                    p_ref[blk, c:c + HEAD_W] = p.astype(BF16)
                    p_lanes = p if p_lanes is None else p_lanes + p
                l_ref[blk] = alpha * l_ref[blk] + p_lanes
                m_ref[blk] = m_new
                al_ref[blk] = alpha

        def weighted_values(p_ref, al_ref, j, rows):
            blk = slice(*rows)
            acc_ref[blk] = (al_ref[blk] * acc_ref[blk]
                            + _dot(p_ref[blk], v_ref[0, head, key_slice(j), :]))

        def diagonal():
            m_ref[...] = jnp.full(m_ref.shape, MASK_VALUE, F32)
            l_ref[...] = jnp.zeros(l_ref.shape, F32)
            acc_ref[...] = jnp.zeros(acc_ref.shape, F32)
            scores(s_a, d0 + 1, upper_rows)
            scores(s_b, d0, all_rows)
            softmax(s_a, p_a, al_a, upper_rows, diag=1)
            weighted_values(p_a, al_a, d0 + 1, upper_rows)
            softmax(s_b, p_b, al_b, all_rows, diag=0)
            scores(s_a, 0, all_rows)

        def pair(i, pending, prefetch=True):
            scores(s_b, 2 * i + 1, all_rows)
            weighted_values(p_b, al_b, pending, all_rows)
            softmax(s_a, p_a, al_a, all_rows)
            if prefetch:
                scores(s_a, 2 * i + 2, all_rows)
            weighted_values(p_a, al_a, 2 * i, all_rows)
            softmax(s_b, p_b, al_b, all_rows)
            return 2 * i + 1

        def full_tiles():
            pending = lax.fori_loop(0, qi - 1, pair, d0)

            @pl.when(qi > 0)
            def _():
                pair(qi - 1, pending, prefetch=False)

        def finish():
            weighted_values(p_b, al_b, jnp.where(qi > 0, 2 * qi - 1, d0), all_rows)
            return acc_ref[...] / jnp.sum(l_ref[...], axis=1, keepdims=True)

        return diagonal, full_tiles, finish

    n_bufs = 9
    n_heads = q_ref.shape[1]
    lane = lax.broadcasted_iota(jnp.int32, (tq, HEAD_W), 1)
    chains = []
    for h in range(n_heads):
        if diff:
            for c, keep in enumerate((lane < DIFF_HEAD_DIM, lane >= DIFF_HEAD_DIM)):
                n = 2 * h + c
                qs_ref = scratch[2 * n_heads * n_bufs + n]
                qs_ref[...] = jnp.where(keep, q_ref[0, h], jnp.zeros((tq, HEAD_W), BF16))
                chains.append(make_chain(lambda lo, hi, r=qs_ref: r[lo:hi], h,
                                         scratch[n * n_bufs:(n + 1) * n_bufs]))
        else:
            chains.append(make_chain(lambda lo, hi, h=h: q_ref[0, h, lo:hi], h,
                                     scratch[h * n_bufs:(h + 1) * n_bufs]))

    for diagonal, _, _ in chains:
        diagonal()
    for _, full_tiles, _ in chains:
        full_tiles()
    outs = [finish() for _, _, finish in chains]

    if diff:
        lq1_ref, lk1_ref, lq2_ref, lk2_ref, sg_ref = extra
        lam = (jnp.exp(jnp.sum(lq1_ref[...] * lk1_ref[...], axis=-1, keepdims=True))
               - jnp.exp(jnp.sum(lq2_ref[...] * lk2_ref[...], axis=-1, keepdims=True))
               + lambda_init)
        outs = [_rms(outs[2 * h] - lam * outs[2 * h + 1], sg_ref[...]) * (1.0 - lambda_init)
                for h in range(n_heads)]
    for h, o in enumerate(outs):
        o_ref[0, h] = o.astype(BF16)


def _const_spec(shape):
    zeros = (0,) * len(shape)
    return pl.BlockSpec(shape, lambda *_: zeros, pipeline_mode=pl.Buffered(1))


def _token_specs(b_s, widths):
    del b_s
    return [pl.BlockSpec((1, TOKEN_TILE, w), lambda b, i: (b, i, 0)) for w in widths]


def _head_spec(width):
    return pl.BlockSpec((1, N_HEADS, TOKEN_TILE, width), lambda b, i: (b, 0, i, 0))


def _params():
    return pltpu.CompilerParams(dimension_semantics=("arbitrary", "arbitrary"),
                                vmem_limit_bytes=VMEM_LIMIT_BYTES)


def _layer_spec(shape, layer):
    return pl.BlockSpec((None,) + shape, lambda *_: (layer, 0, 0), pipeline_mode=pl.Buffered(1))


def _ffn_specs(layer):
    return [_const_spec((1, D_MODEL)), _layer_spec((D_MODEL, D_FF), layer),
            _layer_spec((D_MODEL, D_FF), layer), _layer_spec((D_FF, D_MODEL), layer)]


def _pre_diff(x, phases, ffn, mix_g, w_in):
    b, s, _ = x.shape
    head = jax.ShapeDtypeStruct((b, N_HEADS, s, HEAD_W), BF16)
    return pl.pallas_call(
        _pre_diff_kernel,
        grid=(b, s // TOKEN_TILE),
        in_specs=_token_specs((b, s), [D_MODEL, ROPE_DIM // 2, ROPE_DIM // 2]) + _ffn_specs(ffn[0])
        + [_const_spec((1, D_MODEL)), _const_spec((D_MODEL, 3 * D_MODEL))],
        out_specs=_token_specs((b, s), [D_MODEL]) + [_head_spec(HEAD_W)] * 3,
        out_shape=[jax.ShapeDtypeStruct(x.shape, F32), head, head, head],
        compiler_params=_params(),
        name="pre_diff",
    )(x, *phases, *ffn[1:], mix_g, w_in)


def _pre_mla(x, phases, ffn, mix_g, w_in, q_norm, w_q, kv_norm, w_kv):
    b, s, _ = x.shape
    head = jax.ShapeDtypeStruct((b, N_HEADS, s, HEAD_W), BF16)
    head_qk = jax.ShapeDtypeStruct((b, N_HEADS, s, MLA_QK_W), BF16)
    return pl.pallas_call(
        _pre_mla_kernel,
        grid=(b, s // TOKEN_TILE),
        in_specs=_token_specs((b, s), [D_MODEL, ROPE_DIM // 2, ROPE_DIM // 2]) + _ffn_specs(ffn[0])
        + [_const_spec((1, D_MODEL)), _const_spec(w_in.shape), _const_spec(q_norm.shape),
           _const_spec(w_q.shape), _const_spec(kv_norm.shape), _const_spec(w_kv.shape)],
        out_specs=_token_specs((b, s), [D_MODEL])
        + [_head_spec(MLA_QK_W), _head_spec(MLA_QK_W), _head_spec(HEAD_W)],
        out_shape=[jax.ShapeDtypeStruct(x.shape, F32), head_qk, head_qk, head],
        compiler_params=_params(),
        name="pre_mla",
    )(x, *phases, *ffn[1:], mix_g, w_in, q_norm, w_q, kv_norm, w_kv)


def _post(x, o, w_out, ffn, final_gain=None):
    b, s, _ = x.shape
    final_norm = final_gain is not None
    extra = [final_gain] if final_norm else []
    return pl.pallas_call(
        functools.partial(_post_kernel, final_norm=final_norm),
        grid=(b, s // TOKEN_TILE),
        in_specs=_token_specs((b, s), [D_MODEL]) + [_head_spec(HEAD_W), _const_spec(w_out.shape)]
        + _ffn_specs(ffn[0]) + [_const_spec((1, D_MODEL))] * len(extra),
        out_specs=_token_specs((b, s), [D_MODEL])[0],
        out_shape=jax.ShapeDtypeStruct(x.shape, F32),
        compiler_params=_params(),
        name="post_final" if final_norm else "post",
    )(x, o, w_out, *ffn[1:], *extra)


def _attention(q, k, v, diff_params=None, lambda_init=0.0):
    b, h, s, dqk = q.shape
    tq, tk = ATTN_Q_TILE, ATTN_K_TILE
    diff = diff_params is not None
    extra = list(diff_params) if diff else []
    heads = 2
    n_chains = heads * (2 if diff else 1)

    def chain_scratch():
        return ([pltpu.VMEM((tq, tk), F32), pltpu.VMEM((tq, tk), F32),
                 pltpu.VMEM((tq, tk), BF16), pltpu.VMEM((tq, tk), BF16)]
                + [pltpu.VMEM((tq, HEAD_W), F32) for _ in range(5)])

    return pl.pallas_call(
        functools.partial(_attn_kernel, tq=tq, tk=tk, diff=diff, lambda_init=lambda_init),
        grid=(b, h // heads, s // tq),
        in_specs=[pl.BlockSpec((1, heads, tq, dqk), lambda bi, hi, qi: (bi, hi, qi, 0)),
                  pl.BlockSpec((1, heads, s, dqk), lambda bi, hi, qi: (bi, hi, 0, 0)),
                  pl.BlockSpec((1, heads, s, HEAD_W), lambda bi, hi, qi: (bi, hi, 0, 0))]
        + [pl.BlockSpec(e.shape, lambda bi, hi, qi: (0, 0)) for e in extra],
        out_specs=pl.BlockSpec((1, heads, tq, HEAD_W), lambda bi, hi, qi: (bi, hi, qi, 0)),
        out_shape=jax.ShapeDtypeStruct((b, h, s, HEAD_W), BF16),
        scratch_shapes=[buf for _ in range(n_chains) for buf in chain_scratch()]
        + ([pltpu.VMEM((tq, dqk), BF16) for _ in range(n_chains)] if diff else []),
        compiler_params=pltpu.CompilerParams(
            dimension_semantics=("arbitrary", "arbitrary", "arbitrary"),
            vmem_limit_bytes=VMEM_LIMIT_BYTES),
        name="attn_diff" if diff else "attn_mla",
    )(q, k, v, *extra)


def _rope_phases(positions):
    b, s = positions.shape
    half = ROPE_DIM // 2
    per_row = HEAD_W // half
    inv_freq = ROPE_THETA ** (-jnp.arange(0, ROPE_DIM, 2, dtype=F32) / ROPE_DIM)
    ang = positions.astype(F32).reshape(b, s // per_row, per_row, 1) * inv_freq
    ang = ang.reshape(b, s // per_row, HEAD_W)
    return jnp.cos(ang).reshape(b, s, half), jnp.sin(ang).reshape(b, s, half)


def _row(v):
    return v.reshape(1, -1).astype(F32)


def kernel(x, positions, ffn1_norm, ffn1_w_gate, ffn1_w_up, ffn1_w_down, mix_norm, ffn2_norm, ffn2_w_gate, ffn2_w_up, ffn2_w_down, diff_w_in, diff_lambda_q1, diff_lambda_k1, diff_lambda_q2, diff_lambda_k2, diff_sub_norm, diff_w_out, mla_w_in, mla_q_norm, mla_w_q_up, mla_kv_norm, mla_w_kv_up, mla_w_out, final_norm):
    depth = ffn1_norm.shape[0]
    assert depth == 2 and diff_w_in.shape[0] == 1 and mla_w_in.shape[0] == 1
    phases = _rope_phases(positions)

    def ffn(norm, wg, wu, wd):
        wg, wu, wd = wg.astype(BF16), wu.astype(BF16), wd.astype(BF16)
        return [(i, _row(norm[i]), wg, wu, wd) for i in range(depth)]

    ffn1 = ffn(ffn1_norm, ffn1_w_gate, ffn1_w_up, ffn1_w_down)
    ffn2 = ffn(ffn2_norm, ffn2_w_gate, ffn2_w_up, ffn2_w_down)

    lambda_init = 0.8 - 0.6 * math.exp(-0.3 * 0)
    x, q, k, v = _pre_diff(x, phases, ffn1[0], _row(mix_norm[0]), diff_w_in[0].astype(BF16))
    diff_params = (_row(diff_lambda_q1[0]), _row(diff_lambda_k1[0]), _row(diff_lambda_q2[0]),
                   _row(diff_lambda_k2[0]), _row(diff_sub_norm[0]))
    o = _attention(q, k, v, diff_params, lambda_init)
    x = _post(x, o, diff_w_out[0].astype(BF16), ffn2[0])

    w_in = jnp.pad(mla_w_in[0], ((0, 0), (0, HEAD_W - MLA_ROPE))).astype(BF16)
    w_q = mla_w_q_up[0].reshape(MLA_Q_RANK, N_HEADS, MLA_NOPE + MLA_ROPE)
    w_q = jnp.pad(w_q, ((0, 0), (0, 0), (0, MLA_QK_W - MLA_NOPE - MLA_ROPE)))
    w_q = w_q.reshape(MLA_Q_RANK, N_HEADS * MLA_QK_W).astype(BF16)
    x, q, k, v = _pre_mla(x, phases, ffn1[1], _row(mix_norm[1]), w_in, _row(mla_q_norm[0]), w_q,
                          _row(mla_kv_norm[0]), mla_w_kv_up[0].astype(BF16))
    o = _attention(q, k, v)
    return _post(x, o, mla_w_out[0].astype(BF16), ffn2[1], _row(final_norm))
```

```python
import functools
import math

import jax
import jax.numpy as jnp
from jax import lax
from jax.experimental import pallas as pl
from jax.experimental.pallas import tpu as pltpu

D_MODEL = 1024
D_FF = 2816
N_HEADS = 8
HEAD_W = 128
ROPE_DIM = 64
ROPE_THETA = 10000.0
NORM_EPS = 1e-6
FFN_RESIDUAL_WEIGHT = 0.5
MLA_NOPE = 128
MLA_ROPE = 64
MLA_Q_RANK = 384
MLA_KV_RANK = 256
MLA_QK_W = 256
DIFF_HEAD_DIM = 64

TOKEN_TILE = 512
MXU_WIDTH = 256
FF_CHUNKS = ((0, 6 * MXU_WIDTH), (6 * MXU_WIDTH, D_FF))
ATTN_Q_TILE = 1024
ATTN_K_TILE = ATTN_Q_TILE // 2
VMEM_LIMIT_BYTES = 56 * 1024 * 1024
PHASE_ROWS = 1024
SOFTMAX_ROWS = 64
MASK_VALUE = -1e30
LOG2_E = math.log2(math.e)

F32 = jnp.float32
BF16 = jnp.bfloat16


def _dot(a, b):
    return jnp.dot(a, b, preferred_element_type=F32)


def _rms(x, gain):
    return x * lax.rsqrt(jnp.mean(x * x, axis=-1, keepdims=True) + NORM_EPS) * gain


def _ffn_residual(x, g_ref, wg_ref, wu_ref, wd_ref):
    h = _rms(x, g_ref[...]).astype(BF16)
    out = None
    for lo, hi in FF_CHUNKS:
        gate = _dot(h, wg_ref[:, lo:hi])
        up = _dot(h, wu_ref[:, lo:hi])
        act = (gate * jax.nn.sigmoid(gate) * up).astype(BF16)
        part = _dot(act, wd_ref[lo:hi, :])
        out = part if out is None else out + part
    return x + FFN_RESIDUAL_WEIGHT * out


def _rope(x, cos_f, sin_a, sin_b):
    return (x * cos_f + pltpu.roll(x, HEAD_W - ROPE_DIM // 2, axis=1) * sin_a
            + pltpu.roll(x, ROPE_DIM // 2, axis=1) * sin_b)


def _rope_tables(cos, sin):
    zero = jnp.zeros_like(sin)
    return (jnp.concatenate([cos, cos, cos, cos], axis=1),
            jnp.concatenate([-sin, zero, -sin, zero], axis=1),
            jnp.concatenate([zero, sin, zero, sin], axis=1))


def _pre_diff_kernel(x_ref, cos_ref, sin_ref, fg_ref, wg_ref, wu_ref, wd_ref,
                     mg_ref, win_ref, xo_ref, q_ref, k_ref, v_ref):
    x = _ffn_residual(x_ref[0], fg_ref, wg_ref, wu_ref, wd_ref)
    xo_ref[0] = x
    h = _rms(x, mg_ref[...]).astype(BF16)
    qkv = _dot(h, win_ref[...])
    cos_f, sin_a, sin_b = _rope_tables(cos_ref[0], sin_ref[0])
    scale = DIFF_HEAD_DIM ** -0.5 * LOG2_E
    for hd in range(N_HEADS):
        lo = hd * HEAD_W
        q = _rope(qkv[:, lo:lo + HEAD_W], cos_f, sin_a, sin_b) * scale
        k = _rope(qkv[:, D_MODEL + lo:D_MODEL + lo + HEAD_W], cos_f, sin_a, sin_b)
        q_ref[0, hd] = q.astype(BF16)
        k_ref[0, hd] = k.astype(BF16)
        v_ref[0, hd] = qkv[:, 2 * D_MODEL + lo:2 * D_MODEL + lo + HEAD_W].astype(BF16)


def _pre_mla_kernel(x_ref, cos_ref, sin_ref, fg_ref, wg_ref, wu_ref, wd_ref,
                    mg_ref, win_ref, qn_ref, wq_ref, kvn_ref, wkv_ref,
                    xo_ref, q_ref, k_ref, v_ref):
    x = _ffn_residual(x_ref[0], fg_ref, wg_ref, wu_ref, wd_ref)
    xo_ref[0] = x
    h = _rms(x, mg_ref[...]).astype(BF16)
    c = _dot(h, win_ref[...])
    cos_f, sin_a, sin_b = _rope_tables(cos_ref[0], sin_ref[0])
    k_rope = _rope(c[:, MLA_Q_RANK + MLA_KV_RANK:], cos_f, sin_a, sin_b).astype(BF16)
    cq = _rms(c[:, :MLA_Q_RANK], qn_ref[...]).astype(BF16)
    ckv = _rms(c[:, MLA_Q_RANK:MLA_Q_RANK + MLA_KV_RANK], kvn_ref[...]).astype(BF16)
    q = _dot(cq, wq_ref[...])
    kv = _dot(ckv, wkv_ref[...])
    scale = (MLA_NOPE + MLA_ROPE) ** -0.5 * LOG2_E
    for hd in range(N_HEADS):
        lo = hd * MLA_QK_W
        q_ref[0, hd, :, :HEAD_W] = (q[:, lo:lo + HEAD_W] * scale).astype(BF16)
        q_rope = _rope(q[:, lo + HEAD_W:lo + 2 * HEAD_W], cos_f, sin_a, sin_b) * scale
        q_ref[0, hd, :, HEAD_W:] = q_rope.astype(BF16)
        k_ref[0, hd, :, :HEAD_W] = kv[:, lo:lo + HEAD_W].astype(BF16)
        k_ref[0, hd, :, HEAD_W:] = k_rope
        v_ref[0, hd] = kv[:, lo + HEAD_W:lo + 2 * HEAD_W].astype(BF16)


def _post_kernel(*refs, final_norm):
    if final_norm:
        x_ref, o_ref, wo_ref, fg_ref, wg_ref, wu_ref, wd_ref, ng_ref, xo_ref = refs
    else:
        x_ref, o_ref, wo_ref, fg_ref, wg_ref, wu_ref, wd_ref, xo_ref = refs
    o = jnp.concatenate([o_ref[0, hd] for hd in range(N_HEADS)], axis=1)
    x = x_ref[0] + _dot(o, wo_ref[...])
    x = _ffn_residual(x, fg_ref, wg_ref, wu_ref, wd_ref)
    if final_norm:
        x = _rms(x, ng_ref[...])
    xo_ref[0] = x


def _attn_kernel(*refs, tq, tk, diff, lambda_init):
    n_extra = 5 if diff else 0
    q_ref, k_ref, v_ref = refs[:3]
    extra = refs[3:3 + n_extra]
    o_ref = refs[3 + n_extra]
    scratch = refs[4 + n_extra:]
    assert tq == 2 * tk
    all_rows = (0, tq)
    upper_rows = (tk, tq)
    qi = pl.program_id(2)
    d0 = 2 * qi

    def key_slice(j):
        return pl.ds(pl.multiple_of(j * tk, tk), tk)

    def make_chain(q_rows, head, bufs):
        s_a, s_b, p_a, p_b, al_a, al_b, m_ref, l_ref, acc_ref = bufs

        def scores(s_ref, j, rows):
            s_ref[slice(*rows)] = lax.dot_general(
                q_rows(*rows), k_ref[0, head, key_slice(j), :], (((1,), (1,)), ((), ())),
                preferred_element_type=F32)

        def softmax(s_ref, p_ref, al_ref, rows, diag=None):
            for r0 in range(rows[0], rows[1], SOFTMAX_ROWS):
                blk = slice(r0, r0 + SOFTMAX_ROWS)
                s = s_ref[blk]
                if diag is not None:
                    q0 = r0 - diag * tk
                    assert q0 + SOFTMAX_ROWS > 0
                    if q0 < tk - 1:
                        row = lax.broadcasted_iota(jnp.int32, s.shape, 0) + q0
                        col = lax.broadcasted_iota(jnp.int32, s.shape, 1)
                        s = jnp.where(col <= row, s, MASK_VALUE)
                m_old = m_ref[blk]
                m_blk = jnp.max(s, axis=1, keepdims=True)
                m_new = jnp.maximum(m_old, jnp.broadcast_to(m_blk, m_old.shape))
                alpha = jnp.exp2(m_old - m_new)
                p_lanes = None
                for c in range(0, tk, HEAD_W):
                    p = jnp.exp2(s[:, c:c + HEAD_W] - m_new)
                    p_ref[blk, c:c + HEAD_W] = p.astype(BF16)
                    p_lanes = p if p_lanes is None else p_lanes + p
                l_ref[blk] = alpha * l_ref[blk] + p_lanes
                m_ref[blk] = m_new
                al_ref[blk] = alpha

        def weighted_values(p_ref, al_ref, j, rows):
            blk = slice(*rows)
            acc_ref[blk] = (al_ref[blk] * acc_ref[blk]
                            + _dot(p_ref[blk], v_ref[0, head, key_slice(j), :]))

        def diagonal():
            m_ref[...] = jnp.full(m_ref.shape, MASK_VALUE, F32)
            l_ref[...] = jnp.zeros(l_ref.shape, F32)
            acc_ref[...] = jnp.zeros(acc_ref.shape, F32)
            scores(s_a, d0 + 1, upper_rows)
            scores(s_b, d0, all_rows)
            softmax(s_a, p_a, al_a, upper_rows, diag=1)
            weighted_values(p_a, al_a, d0 + 1, upper_rows)
            softmax(s_b, p_b, al_b, all_rows, diag=0)
            scores(s_a, 0, all_rows)

        def pair(i, pending, prefetch=True):
            scores(s_b, 2 * i + 1, all_rows)
            weighted_values(p_b, al_b, pending, all_rows)
            softmax(s_a, p_a, al_a, all_rows)
            if prefetch:
                scores(s_a, 2 * i + 2, all_rows)
            weighted_values(p_a, al_a, 2 * i, all_rows)
            softmax(s_b, p_b, al_b, all_rows)
            return 2 * i + 1

        def full_tiles():
            pending = lax.fori_loop(0, qi - 1, pair, d0)

            @pl.when(qi > 0)
            def _():
                pair(qi - 1, pending, prefetch=False)

        def finish():
            weighted_values(p_b, al_b, jnp.where(qi > 0, 2 * qi - 1, d0), all_rows)
            return acc_ref[...] / jnp.sum(l_ref[...], axis=1, keepdims=True)

        return diagonal, full_tiles, finish

    n_bufs = 9
    n_heads = q_ref.shape[1]
    lane = lax.broadcasted_iota(jnp.int32, (tq, HEAD_W), 1)
    chains = []
    for h in range(n_heads):
        if diff:
            for c, keep in enumerate((lane < DIFF_HEAD_DIM, lane >= DIFF_HEAD_DIM)):
                n = 2 * h + c
                qs_ref = scratch[2 * n_heads * n_bufs + n]
                qs_ref[...] = jnp.where(keep, q_ref[0, h], jnp.zeros((tq, HEAD_W), BF16))
                chains.append(make_chain(lambda lo, hi, r=qs_ref: r[lo:hi], h,
                                         scratch[n * n_bufs:(n + 1) * n_bufs]))
        else:
            chains.append(make_chain(lambda lo, hi, h=h: q_ref[0, h, lo:hi], h,
                                     scratch[h * n_bufs:(h + 1) * n_bufs]))

    for diagonal, _, _ in chains:
        diagonal()
    for _, full_tiles, _ in chains:
        full_tiles()
    outs = [finish() for _, _, finish in chains]

    if diff:
        lq1_ref, lk1_ref, lq2_ref, lk2_ref, sg_ref = extra
        lam = (jnp.exp(jnp.sum(lq1_ref[...] * lk1_ref[...], axis=-1, keepdims=True))
               - jnp.exp(jnp.sum(lq2_ref[...] * lk2_ref[...], axis=-1, keepdims=True))
               + lambda_init)
        outs = [_rms(outs[2 * h] - lam * outs[2 * h + 1], sg_ref[...]) * (1.0 - lambda_init)
                for h in range(n_heads)]
    for h, o in enumerate(outs):
        o_ref[0, h] = o.astype(BF16)


def _const_spec(shape):
    zeros = (0,) * len(shape)
    return pl.BlockSpec(shape, lambda *_: zeros, pipeline_mode=pl.Buffered(1))


def _token_specs(b_s, widths):
    del b_s
    return [pl.BlockSpec((1, TOKEN_TILE, w), lambda b, i: (b, i, 0)) for w in widths]


def _head_spec(width):
    return pl.BlockSpec((1, N_HEADS, TOKEN_TILE, width), lambda b, i: (b, 0, i, 0))


def _params():
    return pltpu.CompilerParams(dimension_semantics=("arbitrary", "arbitrary"),
                                vmem_limit_bytes=VMEM_LIMIT_BYTES)


def _layer_spec(shape, layer):
    return pl.BlockSpec((None,) + shape, lambda *_: (layer, 0, 0), pipeline_mode=pl.Buffered(1))


def _ffn_specs(layer):
    return [_const_spec((1, D_MODEL)), _layer_spec((D_MODEL, D_FF), layer),
            _layer_spec((D_MODEL, D_FF), layer), _layer_spec((D_FF, D_MODEL), layer)]


def _pre_diff(x, phases, ffn, mix_g, w_in):
    b, s, _ = x.shape
    head = jax.ShapeDtypeStruct((b, N_HEADS, s, HEAD_W), BF16)
    return pl.pallas_call(
        _pre_diff_kernel,
        grid=(b, s // TOKEN_TILE),
        in_specs=_token_specs((b, s), [D_MODEL, ROPE_DIM // 2, ROPE_DIM // 2]) + _ffn_specs(ffn[0])
        + [_const_spec((1, D_MODEL)), _const_spec((D_MODEL, 3 * D_MODEL))],
        out_specs=_token_specs((b, s), [D_MODEL]) + [_head_spec(HEAD_W)] * 3,
        out_shape=[jax.ShapeDtypeStruct(x.shape, F32), head, head, head],
        compiler_params=_params(),
        name="pre_diff",
    )(x, *phases, *ffn[1:], mix_g, w_in)


def _pre_mla(x, phases, ffn, mix_g, w_in, q_norm, w_q, kv_norm, w_kv):
    b, s, _ = x.shape
    head = jax.ShapeDtypeStruct((b, N_HEADS, s, HEAD_W), BF16)
    head_qk = jax.ShapeDtypeStruct((b, N_HEADS, s, MLA_QK_W), BF16)
    return pl.pallas_call(
        _pre_mla_kernel,
        grid=(b, s // TOKEN_TILE),
        in_specs=_token_specs((b, s), [D_MODEL, ROPE_DIM // 2, ROPE_DIM // 2]) + _ffn_specs(ffn[0])
        + [_const_spec((1, D_MODEL)), _const_spec(w_in.shape), _const_spec(q_norm.shape),
           _const_spec(w_q.shape), _const_spec(kv_norm.shape), _const_spec(w_kv.shape)],
        out_specs=_token_specs((b, s), [D_MODEL])
        + [_head_spec(MLA_QK_W), _head_spec(MLA_QK_W), _head_spec(HEAD_W)],
        out_shape=[jax.ShapeDtypeStruct(x.shape, F32), head_qk, head_qk, head],
        compiler_params=_params(),
        name="pre_mla",
    )(x, *phases, *ffn[1:], mix_g, w_in, q_norm, w_q, kv_norm, w_kv)


def _post(x, o, w_out, ffn, final_gain=None):
    b, s, _ = x.shape
    final_norm = final_gain is not None
    extra = [final_gain] if final_norm else []
    return pl.pallas_call(
        functools.partial(_post_kernel, final_norm=final_norm),
        grid=(b, s // TOKEN_TILE),
        in_specs=_token_specs((b, s), [D_MODEL]) + [_head_spec(HEAD_W), _const_spec(w_out.shape)]
        + _ffn_specs(ffn[0]) + [_const_spec((1, D_MODEL))] * len(extra),
        out_specs=_token_specs((b, s), [D_MODEL])[0],
        out_shape=jax.ShapeDtypeStruct(x.shape, F32),
        compiler_params=_params(),
        name="post_final" if final_norm else "post",
    )(x, o, w_out, *ffn[1:], *extra)


def _attention(q, k, v, diff_params=None, lambda_init=0.0):
    b, h, s, dqk = q.shape
    tq, tk = ATTN_Q_TILE, ATTN_K_TILE
    diff = diff_params is not None
    extra = list(diff_params) if diff else []
    heads = 2
    n_chains = heads * (2 if diff else 1)

    def chain_scratch():
        return ([pltpu.VMEM((tq, tk), F32), pltpu.VMEM((tq, tk), F32),
                 pltpu.VMEM((tq, tk), BF16), pltpu.VMEM((tq, tk), BF16)]
                + [pltpu.VMEM((tq, HEAD_W), F32) for _ in range(5)])

    return pl.pallas_call(
        functools.partial(_attn_kernel, tq=tq, tk=tk, diff=diff, lambda_init=lambda_init),
        grid=(b, h // heads, s // tq),
        in_specs=[pl.BlockSpec((1, heads, tq, dqk), lambda bi, hi, qi: (bi, hi, qi, 0)),
                  pl.BlockSpec((1, heads, s, dqk), lambda bi, hi, qi: (bi, hi, 0, 0)),
                  pl.BlockSpec((1, heads, s, HEAD_W), lambda bi, hi, qi: (bi, hi, 0, 0))]
        + [pl.BlockSpec(e.shape, lambda bi, hi, qi: (0, 0)) for e in extra],
        out_specs=pl.BlockSpec((1, heads, tq, HEAD_W), lambda bi, hi, qi: (bi, hi, qi, 0)),
        out_shape=jax.ShapeDtypeStruct((b, h, s, HEAD_W), BF16),
        scratch_shapes=[buf for _ in range(n_chains) for buf in chain_scratch()]
        + ([pltpu.VMEM((tq, dqk), BF16) for _ in range(n_chains)] if diff else []),
        compiler_params=pltpu.CompilerParams(
            dimension_semantics=("arbitrary", "arbitrary", "arbitrary"),
            vmem_limit_bytes=VMEM_LIMIT_BYTES),
        name="attn_diff" if diff else "attn_mla",
    )(q, k, v, *extra)


def _phase_kernel(pos_ref, freq_ref, cos_ref, sin_ref):
    ang = pos_ref[...] * freq_ref[...]
    cos_ref[...] = jnp.cos(ang)
    sin_ref[...] = jnp.sin(ang)


def _rope_phases(positions):
    b, s = positions.shape
    half = ROPE_DIM // 2
    per_row = HEAD_W // half
    rows = b * s // per_row
    inv_freq = ROPE_THETA ** (-jnp.arange(0, ROPE_DIM, 2, dtype=F32) / ROPE_DIM)
    pos = jnp.repeat(positions.astype(F32).reshape(rows, per_row), half, axis=1)
    spec = pl.BlockSpec((PHASE_ROWS, HEAD_W), lambda i: (i, 0))
    out = jax.ShapeDtypeStruct((rows, HEAD_W), F32)
    cos, sin = pl.pallas_call(
        _phase_kernel,
        grid=(rows // PHASE_ROWS,),
        in_specs=[spec, pl.BlockSpec((1, HEAD_W), lambda i: (0, 0))],
        out_specs=[spec, spec],
        out_shape=[out, out],
        name="rope_phases",
    )(pos, jnp.tile(inv_freq, per_row).reshape(1, HEAD_W))
    return cos.reshape(b, s, half), sin.reshape(b, s, half)


def _row(v):
    return v.reshape(1, -1).astype(F32)


def kernel(x, positions, ffn1_norm, ffn1_w_gate, ffn1_w_up, ffn1_w_down, mix_norm, ffn2_norm, ffn2_w_gate, ffn2_w_up, ffn2_w_down, diff_w_in, diff_lambda_q1, diff_lambda_k1, diff_lambda_q2, diff_lambda_k2, diff_sub_norm, diff_w_out, mla_w_in, mla_q_norm, mla_w_q_up, mla_kv_norm, mla_w_kv_up, mla_w_out, final_norm):
    depth = ffn1_norm.shape[0]
    assert depth == 2 and diff_w_in.shape[0] == 1 and mla_w_in.shape[0] == 1
    phases = _rope_phases(positions)

    def ffn(norm, wg, wu, wd):
        wg, wu, wd = wg.astype(BF16), wu.astype(BF16), wd.astype(BF16)
        return [(i, _row(norm[i]), wg, wu, wd) for i in range(depth)]

    ffn1 = ffn(ffn1_norm, ffn1_w_gate, ffn1_w_up, ffn1_w_down)
    ffn2 = ffn(ffn2_norm, ffn2_w_gate, ffn2_w_up, ffn2_w_down)

    lambda_init = 0.8 - 0.6 * math.exp(-0.3 * 0)
    x, q, k, v = _pre_diff(x, phases, ffn1[0], _row(mix_norm[0]), diff_w_in[0].astype(BF16))
    diff_params = (_row(diff_lambda_q1[0]), _row(diff_lambda_k1[0]), _row(diff_lambda_q2[0]),
                   _row(diff_lambda_k2[0]), _row(diff_sub_norm[0]))
    o = _attention(q, k, v, diff_params, lambda_init)
    x = _post(x, o, diff_w_out[0].astype(BF16), ffn2[0])

    w_in = jnp.pad(mla_w_in[0], ((0, 0), (0, HEAD_W - MLA_ROPE))).astype(BF16)
    w_q = mla_w_q_up[0].reshape(MLA_Q_RANK, N_HEADS, MLA_NOPE + MLA_ROPE)
    w_q = jnp.pad(w_q, ((0, 0), (0, 0), (0, MLA_QK_W - MLA_NOPE - MLA_ROPE)))
    w_q = w_q.reshape(MLA_Q_RANK, N_HEADS * MLA_QK_W).astype(BF16)
    x, q, k, v = _pre_mla(x, phases, ffn1[1], _row(mix_norm[1]), w_in, _row(mla_q_norm[0]), w_q,
                          _row(mla_kv_norm[0]), mla_w_kv_up[0].astype(BF16))
    o = _attention(q, k, v)
    return _post(x, o, mla_w_out[0].astype(BF16), ffn2[1], _row(final_norm))
```

```python
import functools
import math

import jax
import jax.numpy as jnp
from jax import lax
from jax.experimental import pallas as pl
from jax.experimental.pallas import tpu as pltpu

D_MODEL = 1024
D_FF = 2816
N_HEADS = 8
HEAD_W = 128
ROPE_DIM = 64
ROPE_THETA = 10000.0
NORM_EPS = 1e-6
FFN_RESIDUAL_WEIGHT = 0.5
MLA_NOPE = 128
MLA_ROPE = 64
MLA_Q_RANK = 384
MLA_KV_RANK = 256
MLA_QK_W = 256
DIFF_HEAD_DIM = 64

TOKEN_TILE = 512
MXU_WIDTH = 256
FF_CHUNKS = ((0, 6 * MXU_WIDTH), (6 * MXU_WIDTH, D_FF))
ATTN_Q_TILE = 1024
ATTN_K_TILE = ATTN_Q_TILE // 2
VMEM_LIMIT_BYTES = 56 * 1024 * 1024
PHASE_ROWS = 1024
SOFTMAX_ROWS = 64
MASK_VALUE = -1e30
LOG2_E = math.log2(math.e)

F32 = jnp.float32
BF16 = jnp.bfloat16


def _dot(a, b):
    return jnp.dot(a, b, preferred_element_type=F32)


def _rms(x, gain):
    return x * lax.rsqrt(jnp.mean(x * x, axis=-1, keepdims=True) + NORM_EPS) * gain


def _ffn_residual(x, g_ref, wg_ref, wu_ref, wd_ref):
    h = _rms(x, g_ref[...]).astype(BF16)
    out = None
    for lo, hi in FF_CHUNKS:
        gate = _dot(h, wg_ref[:, lo:hi])
        up = _dot(h, wu_ref[:, lo:hi])
        act = (gate * jax.nn.sigmoid(gate) * up).astype(BF16)
        part = _dot(act, wd_ref[lo:hi, :])
        out = part if out is None else out + part
    return x + FFN_RESIDUAL_WEIGHT * out


def _rope(x, cos_f, sin_a, sin_b):
    return (x * cos_f + pltpu.roll(x, HEAD_W - ROPE_DIM // 2, axis=1) * sin_a
            + pltpu.roll(x, ROPE_DIM // 2, axis=1) * sin_b)


def _rope_tables(cos, sin):
    zero = jnp.zeros_like(sin)
    return (jnp.concatenate([cos, cos, cos, cos], axis=1),
            jnp.concatenate([-sin, zero, -sin, zero], axis=1),
            jnp.concatenate([zero, sin, zero, sin], axis=1))


def _pre_diff_kernel(x_ref, cos_ref, sin_ref, fg_ref, wg_ref, wu_ref, wd_ref,
                     mg_ref, win_ref, xo_ref, q_ref, k_ref, v_ref):
    x = _ffn_residual(x_ref[0], fg_ref, wg_ref, wu_ref, wd_ref)
    xo_ref[0] = x
    h = _rms(x, mg_ref[...]).astype(BF16)
    qkv = _dot(h, win_ref[...])
    cos_f, sin_a, sin_b = _rope_tables(cos_ref[0], sin_ref[0])
    scale = DIFF_HEAD_DIM ** -0.5 * LOG2_E
    for hd in range(N_HEADS):
        lo = hd * HEAD_W
        q = _rope(qkv[:, lo:lo + HEAD_W], cos_f, sin_a, sin_b) * scale
        k = _rope(qkv[:, D_MODEL + lo:D_MODEL + lo + HEAD_W], cos_f, sin_a, sin_b)
        q_ref[0, hd] = q.astype(BF16)
        k_ref[0, hd] = k.astype(BF16)
        v_ref[0, hd] = qkv[:, 2 * D_MODEL + lo:2 * D_MODEL + lo + HEAD_W].astype(BF16)


def _pre_mla_kernel(x_ref, cos_ref, sin_ref, fg_ref, wg_ref, wu_ref, wd_ref,
                    mg_ref, win_ref, qn_ref, wq_ref, kvn_ref, wkv_ref,
                    xo_ref, q_ref, k_ref, v_ref):
    x = _ffn_residual(x_ref[0], fg_ref, wg_ref, wu_ref, wd_ref)
    xo_ref[0] = x
    h = _rms(x, mg_ref[...]).astype(BF16)
    c = _dot(h, win_ref[...])
    cos_f, sin_a, sin_b = _rope_tables(cos_ref[0], sin_ref[0])
    k_rope = _rope(c[:, MLA_Q_RANK + MLA_KV_RANK:], cos_f, sin_a, sin_b).astype(BF16)
    cq = _rms(c[:, :MLA_Q_RANK], qn_ref[...]).astype(BF16)
    ckv = _rms(c[:, MLA_Q_RANK:MLA_Q_RANK + MLA_KV_RANK], kvn_ref[...]).astype(BF16)
    q = _dot(cq, wq_ref[...])
    kv = _dot(ckv, wkv_ref[...])
    scale = (MLA_NOPE + MLA_ROPE) ** -0.5 * LOG2_E
    for hd in range(N_HEADS):
        lo = hd * MLA_QK_W
        q_ref[0, hd, :, :HEAD_W] = (q[:, lo:lo + HEAD_W] * scale).astype(BF16)
        q_rope = _rope(q[:, lo + HEAD_W:lo + 2 * HEAD_W], cos_f, sin_a, sin_b) * scale
        q_ref[0, hd, :, HEAD_W:] = q_rope.astype(BF16)
        k_ref[0, hd, :, :HEAD_W] = kv[:, lo:lo + HEAD_W].astype(BF16)
        k_ref[0, hd, :, HEAD_W:] = k_rope
        v_ref[0, hd] = kv[:, lo + HEAD_W:lo + 2 * HEAD_W].astype(BF16)


def _post_kernel(*refs, final_norm):
    if final_norm:
        x_ref, o_ref, wo_ref, fg_ref, wg_ref, wu_ref, wd_ref, ng_ref, xo_ref = refs
    else:
        x_ref, o_ref, wo_ref, fg_ref, wg_ref, wu_ref, wd_ref, xo_ref = refs
    o = jnp.concatenate([o_ref[0, hd] for hd in range(N_HEADS)], axis=1)
    x = x_ref[0] + _dot(o, wo_ref[...])
    x = _ffn_residual(x, fg_ref, wg_ref, wu_ref, wd_ref)
    if final_norm:
        x = _rms(x, ng_ref[...])
    xo_ref[0] = x


def _attn_kernel(*refs, tq, tk, diff, lambda_init):
    n_extra = 5 if diff else 0
    q_ref, k_ref, v_ref = refs[:3]
    extra = refs[3:3 + n_extra]
    o_ref = refs[3 + n_extra]
    scratch = refs[4 + n_extra:]
    assert tq == 2 * tk
    all_rows = (0, tq)
    upper_rows = (tk, tq)
    qi = pl.program_id(2)
    d0 = 2 * qi

    def key_slice(j):
        return pl.ds(pl.multiple_of(j * tk, tk), tk)

    def make_chain(q_rows, head, bufs):
        s_a, s_b, p_a, p_b, al_a, al_b, m_ref, l_ref, acc_ref = bufs

        def scores(s_ref, j, rows):
            s_ref[slice(*rows)] = lax.dot_general(
                q_rows(*rows), k_ref[0, head, key_slice(j), :], (((1,), (1,)), ((), ())),
                preferred_element_type=F32)

        def softmax(s_ref, p_ref, al_ref, rows, diag=None):
            for r0 in range(rows[0], rows[1], SOFTMAX_ROWS):
                blk = slice(r0, r0 + SOFTMAX_ROWS)
                s = s_ref[blk]
                if diag is not None:
                    q0 = r0 - diag * tk
                    assert q0 + SOFTMAX_ROWS > 0
                    if q0 < tk - 1:
                        row = lax.broadcasted_iota(jnp.int32, s.shape, 0) + q0
                        col = lax.broadcasted_iota(jnp.int32, s.shape, 1)
                        s = jnp.where(col <= row, s, MASK_VALUE)
                m_old = m_ref[blk]
                m_blk = jnp.max(s, axis=1, keepdims=True)
                m_new = jnp.maximum(m_old, jnp.broadcast_to(m_blk, m_old.shape))
                alpha = jnp.exp2(m_old - m_new)
                p_lanes = None
                for c in range(0, tk, HEAD_W):
                    p = jnp.exp2(s[:, c:c + HEAD_W] - m_new)
                    p_ref[blk, c:c + HEAD_W] = p.astype(BF16)
                    p_lanes = p if p_lanes is None else p_lanes + p
                l_ref[blk] = alpha * l_ref[blk] + p_lanes
                m_ref[blk] = m_new
                al_ref[blk] = alpha

        def weighted_values(p_ref, al_ref, j, rows):
            blk = slice(*rows)
            acc_ref[blk] = (al_ref[blk] * acc_ref[blk]
                            + _dot(p_ref[blk], v_ref[0, head, key_slice(j), :]))

        def diagonal():
            m_ref[...] = jnp.full(m_ref.shape, MASK_VALUE, F32)
            l_ref[...] = jnp.zeros(l_ref.shape, F32)
            acc_ref[...] = jnp.zeros(acc_ref.shape, F32)
            scores(s_a, d0 + 1, upper_rows)
            scores(s_b, d0, all_rows)
            softmax(s_a, p_a, al_a, upper_rows, diag=1)
            weighted_values(p_a, al_a, d0 + 1, upper_rows)
            softmax(s_b, p_b, al_b, all_rows, diag=0)
            scores(s_a, 0, all_rows)

        def pair(i, pending, prefetch=True):
            weighted_values(p_b, al_b, pending, all_rows)
            scores(s_b, 2 * i + 1, all_rows)
            softmax(s_a, p_a, al_a, all_rows)
            if prefetch:
                scores(s_a, 2 * i + 2, all_rows)
            weighted_values(p_a, al_a, 2 * i, all_rows)
            softmax(s_b, p_b, al_b, all_rows)
            return 2 * i + 1

        def full_tiles():
            pending = lax.fori_loop(0, qi - 1, pair, d0)

            @pl.when(qi > 0)
            def _():
                pair(qi - 1, pending, prefetch=False)

        def finish():
            weighted_values(p_b, al_b, jnp.where(qi > 0, 2 * qi - 1, d0), all_rows)
            return acc_ref[...] / jnp.sum(l_ref[...], axis=1, keepdims=True)

        return diagonal, full_tiles, finish

    n_bufs = 9
    n_heads = q_ref.shape[1]
    lane = lax.broadcasted_iota(jnp.int32, (tq, HEAD_W), 1)
    chains = []
    for h in range(n_heads):
        if diff:
            for c, keep in enumerate((lane < DIFF_HEAD_DIM, lane >= DIFF_HEAD_DIM)):
                n = 2 * h + c
                qs_ref = scratch[2 * n_heads * n_bufs + n]
                qs_ref[...] = jnp.where(keep, q_ref[0, h], jnp.zeros((tq, HEAD_W), BF16))
                chains.append(make_chain(lambda lo, hi, r=qs_ref: r[lo:hi], h,
                                         scratch[n * n_bufs:(n + 1) * n_bufs]))
        else:
            chains.append(make_chain(lambda lo, hi, h=h: q_ref[0, h, lo:hi], h,
                                     scratch[h * n_bufs:(h + 1) * n_bufs]))

    for diagonal, _, _ in chains:
        diagonal()
    for _, full_tiles, _ in chains:
        full_tiles()
    outs = [finish() for _, _, finish in chains]

    if diff:
        lq1_ref, lk1_ref, lq2_ref, lk2_ref, sg_ref = extra
        lam = (jnp.exp(jnp.sum(lq1_ref[...] * lk1_ref[...], axis=-1, keepdims=True))
               - jnp.exp(jnp.sum(lq2_ref[...] * lk2_ref[...], axis=-1, keepdims=True))
               + lambda_init)
        outs = [_rms(outs[2 * h] - lam * outs[2 * h + 1], sg_ref[...]) * (1.0 - lambda_init)
                for h in range(n_heads)]
    for h, o in enumerate(outs):
        o_ref[0, h] = o.astype(BF16)


def _const_spec(shape):
    zeros = (0,) * len(shape)
    return pl.BlockSpec(shape, lambda *_: zeros, pipeline_mode=pl.Buffered(1))


def _token_specs(b_s, widths):
    del b_s
    return [pl.BlockSpec((1, TOKEN_TILE, w), lambda b, i: (b, i, 0)) for w in widths]


def _head_spec(width):
    return pl.BlockSpec((1, N_HEADS, TOKEN_TILE, width), lambda b, i: (b, 0, i, 0))


def _params():
    return pltpu.CompilerParams(dimension_semantics=("arbitrary", "arbitrary"),
                                vmem_limit_bytes=VMEM_LIMIT_BYTES)


def _layer_spec(shape, layer):
    return pl.BlockSpec((None,) + shape, lambda *_: (layer, 0, 0), pipeline_mode=pl.Buffered(1))


def _ffn_specs(layer):
    return [_const_spec((1, D_MODEL)), _layer_spec((D_MODEL, D_FF), layer),
            _layer_spec((D_MODEL, D_FF), layer), _layer_spec((D_FF, D_MODEL), layer)]


def _pre_diff(x, phases, ffn, mix_g, w_in):
    b, s, _ = x.shape
    head = jax.ShapeDtypeStruct((b, N_HEADS, s, HEAD_W), BF16)
    return pl.pallas_call(
        _pre_diff_kernel,
        grid=(b, s // TOKEN_TILE),
        in_specs=_token_specs((b, s), [D_MODEL, ROPE_DIM // 2, ROPE_DIM // 2]) + _ffn_specs(ffn[0])
        + [_const_spec((1, D_MODEL)), _const_spec((D_MODEL, 3 * D_MODEL))],
        out_specs=_token_specs((b, s), [D_MODEL]) + [_head_spec(HEAD_W)] * 3,
        out_shape=[jax.ShapeDtypeStruct(x.shape, F32), head, head, head],
        compiler_params=_params(),
        name="pre_diff",
    )(x, *phases, *ffn[1:], mix_g, w_in)


def _pre_mla(x, phases, ffn, mix_g, w_in, q_norm, w_q, kv_norm, w_kv):
    b, s, _ = x.shape
    head = jax.ShapeDtypeStruct((b, N_HEADS, s, HEAD_W), BF16)
    head_qk = jax.ShapeDtypeStruct((b, N_HEADS, s, MLA_QK_W), BF16)
    return pl.pallas_call(
        _pre_mla_kernel,
        grid=(b, s // TOKEN_TILE),
        in_specs=_token_specs((b, s), [D_MODEL, ROPE_DIM // 2, ROPE_DIM // 2]) + _ffn_specs(ffn[0])
        + [_const_spec((1, D_MODEL)), _const_spec(w_in.shape), _const_spec(q_norm.shape),
           _const_spec(w_q.shape), _const_spec(kv_norm.shape), _const_spec(w_kv.shape)],
        out_specs=_token_specs((b, s), [D_MODEL])
        + [_head_spec(MLA_QK_W), _head_spec(MLA_QK_W), _head_spec(HEAD_W)],
        out_shape=[jax.ShapeDtypeStruct(x.shape, F32), head_qk, head_qk, head],
        compiler_params=_params(),
        name="pre_mla",
    )(x, *phases, *ffn[1:], mix_g, w_in, q_norm, w_q, kv_norm, w_kv)


def _post(x, o, w_out, ffn, final_gain=None):
    b, s, _ = x.shape
    final_norm = final_gain is not None
    extra = [final_gain] if final_norm else []
    return pl.pallas_call(
        functools.partial(_post_kernel, final_norm=final_norm),
        grid=(b, s // TOKEN_TILE),
        in_specs=_token_specs((b, s), [D_MODEL]) + [_head_spec(HEAD_W), _const_spec(w_out.shape)]
        + _ffn_specs(ffn[0]) + [_const_spec((1, D_MODEL))] * len(extra),
        out_specs=_token_specs((b, s), [D_MODEL])[0],
        out_shape=jax.ShapeDtypeStruct(x.shape, F32),
        compiler_params=_params(),
        name="post_final" if final_norm else "post",
    )(x, o, w_out, *ffn[1:], *extra)


def _attention(q, k, v, diff_params=None, lambda_init=0.0):
    b, h, s, dqk = q.shape
    tq, tk = ATTN_Q_TILE, ATTN_K_TILE
    diff = diff_params is not None
    extra = list(diff_params) if diff else []
    heads = 2
    n_chains = heads * (2 if diff else 1)

    def chain_scratch():
        return ([pltpu.VMEM((tq, tk), F32), pltpu.VMEM((tq, tk), F32),
                 pltpu.VMEM((tq, tk), BF16), pltpu.VMEM((tq, tk), BF16)]
                + [pltpu.VMEM((tq, HEAD_W), F32) for _ in range(5)])

    return pl.pallas_call(
        functools.partial(_attn_kernel, tq=tq, tk=tk, diff=diff, lambda_init=lambda_init),
        grid=(b, h // heads, s // tq),
        in_specs=[pl.BlockSpec((1, heads, tq, dqk), lambda bi, hi, qi: (bi, hi, qi, 0)),
                  pl.BlockSpec((1, heads, s, dqk), lambda bi, hi, qi: (bi, hi, 0, 0)),
                  pl.BlockSpec((1, heads, s, HEAD_W), lambda bi, hi, qi: (bi, hi, 0, 0))]
        + [pl.BlockSpec(e.shape, lambda bi, hi, qi: (0, 0)) for e in extra],
        out_specs=pl.BlockSpec((1, heads, tq, HEAD_W), lambda bi, hi, qi: (bi, hi, qi, 0)),
        out_shape=jax.ShapeDtypeStruct((b, h, s, HEAD_W), BF16),
        scratch_shapes=[buf for _ in range(n_chains) for buf in chain_scratch()]
        + ([pltpu.VMEM((tq, dqk), BF16) for _ in range(n_chains)] if diff else []),
        compiler_params=pltpu.CompilerParams(
            dimension_semantics=("arbitrary", "arbitrary", "arbitrary"),
            vmem_limit_bytes=VMEM_LIMIT_BYTES),
        name="attn_diff" if diff else "attn_mla",
    )(q, k, v, *extra)


def _phase_kernel(pos_ref, freq_ref, cos_ref, sin_ref):
    ang = pos_ref[...] * freq_ref[...]
    cos_ref[...] = jnp.cos(ang)
    sin_ref[...] = jnp.sin(ang)


def _rope_phases(positions):
    b, s = positions.shape
    half = ROPE_DIM // 2
    per_row = HEAD_W // half
    rows = b * s // per_row
    inv_freq = ROPE_THETA ** (-jnp.arange(0, ROPE_DIM, 2, dtype=F32) / ROPE_DIM)
    pos = jnp.repeat(positions.astype(F32).reshape(rows, per_row), half, axis=1)
    spec = pl.BlockSpec((PHASE_ROWS, HEAD_W), lambda i: (i, 0))
    out = jax.ShapeDtypeStruct((rows, HEAD_W), F32)
    cos, sin = pl.pallas_call(
        _phase_kernel,
        grid=(rows // PHASE_ROWS,),
        in_specs=[spec, pl.BlockSpec((1, HEAD_W), lambda i: (0, 0))],
        out_specs=[spec, spec],
        out_shape=[out, out],
        name="rope_phases",
    )(pos, jnp.tile(inv_freq, per_row).reshape(1, HEAD_W))
    return cos.reshape(b, s, half), sin.reshape(b, s, half)


def _row(v):
    return v.reshape(1, -1).astype(F32)


def kernel(x, positions, ffn1_norm, ffn1_w_gate, ffn1_w_up, ffn1_w_down, mix_norm, ffn2_norm, ffn2_w_gate, ffn2_w_up, ffn2_w_down, diff_w_in, diff_lambda_q1, diff_lambda_k1, diff_lambda_q2, diff_lambda_k2, diff_sub_norm, diff_w_out, mla_w_in, mla_q_norm, mla_w_q_up, mla_kv_norm, mla_w_kv_up, mla_w_out, final_norm):
    depth = ffn1_norm.shape[0]
    assert depth == 2 and diff_w_in.shape[0] == 1 and mla_w_in.shape[0] == 1
    phases = _rope_phases(positions)

    def ffn(norm, wg, wu, wd):
        wg, wu, wd = wg.astype(BF16), wu.astype(BF16), wd.astype(BF16)
        return [(i, _row(norm[i]), wg, wu, wd) for i in range(depth)]

    ffn1 = ffn(ffn1_norm, ffn1_w_gate, ffn1_w_up, ffn1_w_down)
    ffn2 = ffn(ffn2_norm, ffn2_w_gate, ffn2_w_up, ffn2_w_down)

    lambda_init = 0.8 - 0.6 * math.exp(-0.3 * 0)
    x, q, k, v = _pre_diff(x, phases, ffn1[0], _row(mix_norm[0]), diff_w_in[0].astype(BF16))
    diff_params = (_row(diff_lambda_q1[0]), _row(diff_lambda_k1[0]), _row(diff_lambda_q2[0]),
                   _row(diff_lambda_k2[0]), _row(diff_sub_norm[0]))
    o = _attention(q, k, v, diff_params, lambda_init)
    x = _post(x, o, diff_w_out[0].astype(BF16), ffn2[0])

    w_in = jnp.pad(mla_w_in[0], ((0, 0), (0, HEAD_W - MLA_ROPE))).astype(BF16)
    w_q = mla_w_q_up[0].reshape(MLA_Q_RANK, N_HEADS, MLA_NOPE + MLA_ROPE)
    w_q = jnp.pad(w_q, ((0, 0), (0, 0), (0, MLA_QK_W - MLA_NOPE - MLA_ROPE)))
    w_q = w_q.reshape(MLA_Q_RANK, N_HEADS * MLA_QK_W).astype(BF16)
    x, q, k, v = _pre_mla(x, phases, ffn1[1], _row(mix_norm[1]), w_in, _row(mla_q_norm[0]), w_q,
                          _row(mla_kv_norm[0]), mla_w_kv_up[0].astype(BF16))
    o = _attention(q, k, v)
    return _post(x, o, mla_w_out[0].astype(BF16), ffn2[1], _row(final_norm))
```

```python
import functools
import math

import jax
import jax.numpy as jnp
from jax import lax
from jax.experimental import pallas as pl
from jax.experimental.pallas import tpu as pltpu

D_MODEL = 1024
D_FF = 2816
N_HEADS = 8
HEAD_W = 128
ROPE_DIM = 64
ROPE_THETA = 10000.0
NORM_EPS = 1e-6
FFN_RESIDUAL_WEIGHT = 0.5
MLA_NOPE = 128
MLA_ROPE = 64
MLA_Q_RANK = 384
MLA_KV_RANK = 256
MLA_QK_W = 256
DIFF_HEAD_DIM = 64

TOKEN_TILE = 512
MXU_WIDTH = 256
FF_CHUNKS = ((0, 6 * MXU_WIDTH), (6 * MXU_WIDTH, D_FF))
ATTN_Q_TILE = 1024
ATTN_K_TILE = ATTN_Q_TILE // 2
VMEM_LIMIT_BYTES = 56 * 1024 * 1024
PHASE_ROWS = 1024
SOFTMAX_ROWS = 64
MASK_VALUE = -1e30
LOG2_E = math.log2(math.e)

F32 = jnp.float32
BF16 = jnp.bfloat16


def _dot(a, b):
    return jnp.dot(a, b, preferred_element_type=F32)


def _rms(x, gain):
    return x * lax.rsqrt(jnp.mean(x * x, axis=-1, keepdims=True) + NORM_EPS) * gain


def _ffn_residual(x, g_ref, wg_ref, wu_ref, wd_ref):
    h = _rms(x, g_ref[...]).astype(BF16)
    out = None
    for lo, hi in FF_CHUNKS:
        gate = _dot(h, wg_ref[:, lo:hi])
        up = _dot(h, wu_ref[:, lo:hi])
        act = (gate * jax.nn.sigmoid(gate) * up).astype(BF16)
        part = _dot(act, wd_ref[lo:hi, :])
        out = part if out is None else out + part
    return x + FFN_RESIDUAL_WEIGHT * out


def _rope(x, cos_f, sin_a, sin_b):
    return (x * cos_f + pltpu.roll(x, HEAD_W - ROPE_DIM // 2, axis=1) * sin_a
            + pltpu.roll(x, ROPE_DIM // 2, axis=1) * sin_b)


def _rope_tables(cos, sin):
    zero = jnp.zeros_like(sin)
    return (jnp.concatenate([cos, cos, cos, cos], axis=1),
            jnp.concatenate([-sin, zero, -sin, zero], axis=1),
            jnp.concatenate([zero, sin, zero, sin], axis=1))


def _pre_diff_kernel(x_ref, cos_ref, sin_ref, fg_ref, wg_ref, wu_ref, wd_ref,
                     mg_ref, win_ref, xo_ref, q_ref, k_ref, v_ref):
    x = _ffn_residual(x_ref[0], fg_ref, wg_ref, wu_ref, wd_ref)
    xo_ref[0] = x
    h = _rms(x, mg_ref[...]).astype(BF16)
    qkv = _dot(h, win_ref[...])
    cos_f, sin_a, sin_b = _rope_tables(cos_ref[0], sin_ref[0])
    scale = DIFF_HEAD_DIM ** -0.5 * LOG2_E
    for hd in range(N_HEADS):
        lo = hd * HEAD_W
        q = _rope(qkv[:, lo:lo + HEAD_W], cos_f, sin_a, sin_b) * scale
        k = _rope(qkv[:, D_MODEL + lo:D_MODEL + lo + HEAD_W], cos_f, sin_a, sin_b)
        q_ref[0, hd] = q.astype(BF16)
        k_ref[0, hd] = k.astype(BF16)
        v_ref[0, hd] = qkv[:, 2 * D_MODEL + lo:2 * D_MODEL + lo + HEAD_W].astype(BF16)


def _pre_mla_kernel(x_ref, cos_ref, sin_ref, fg_ref, wg_ref, wu_ref, wd_ref,
                    mg_ref, win_ref, qn_ref, wq_ref, kvn_ref, wkv_ref,
                    xo_ref, q_ref, k_ref, v_ref):
    x = _ffn_residual(x_ref[0], fg_ref, wg_ref, wu_ref, wd_ref)
    xo_ref[0] = x
    h = _rms(x, mg_ref[...]).astype(BF16)
    c = _dot(h, win_ref[...])
    cos_f, sin_a, sin_b = _rope_tables(cos_ref[0], sin_ref[0])
    k_rope = _rope(c[:, MLA_Q_RANK + MLA_KV_RANK:], cos_f, sin_a, sin_b).astype(BF16)
    cq = _rms(c[:, :MLA_Q_RANK], qn_ref[...]).astype(BF16)
    ckv = _rms(c[:, MLA_Q_RANK:MLA_Q_RANK + MLA_KV_RANK], kvn_ref[...]).astype(BF16)
    q = _dot(cq, wq_ref[...])
    kv = _dot(ckv, wkv_ref[...])
    scale = (MLA_NOPE + MLA_ROPE) ** -0.5 * LOG2_E
    for hd in range(N_HEADS):
        lo = hd * MLA_QK_W
        q_ref[0, hd, :, :HEAD_W] = (q[:, lo:lo + HEAD_W] * scale).astype(BF16)
        q_rope = _rope(q[:, lo + HEAD_W:lo + 2 * HEAD_W], cos_f, sin_a, sin_b) * scale
        q_ref[0, hd, :, HEAD_W:] = q_rope.astype(BF16)
        k_ref[0, hd, :, :HEAD_W] = kv[:, lo:lo + HEAD_W].astype(BF16)
        k_ref[0, hd, :, HEAD_W:] = k_rope
        v_ref[0, hd] = kv[:, lo + HEAD_W:lo + 2 * HEAD_W].astype(BF16)


def _post_kernel(*refs, final_norm):
    if final_norm:
        x_ref, o_ref, wo_ref, fg_ref, wg_ref, wu_ref, wd_ref, ng_ref, xo_ref = refs
    else:
        x_ref, o_ref, wo_ref, fg_ref, wg_ref, wu_ref, wd_ref, xo_ref = refs
    o = jnp.concatenate([o_ref[0, hd] for hd in range(N_HEADS)], axis=1)
    x = x_ref[0] + _dot(o, wo_ref[...])
    x = _ffn_residual(x, fg_ref, wg_ref, wu_ref, wd_ref)
    if final_norm:
        x = _rms(x, ng_ref[...])
    xo_ref[0] = x


def _attn_kernel(*refs, tq, tk, diff, lambda_init):
    n_extra = 5 if diff else 0
    q_ref, k_ref, v_ref = refs[:3]
    extra = refs[3:3 + n_extra]
    o_ref = refs[3 + n_extra]
    scratch = refs[4 + n_extra:]
    assert tq == 2 * tk
    all_rows = (0, tq)
    upper_rows = (tk, tq)
    qi = pl.program_id(2)
    d0 = 2 * qi

    def key_slice(j):
        return pl.ds(pl.multiple_of(j * tk, tk), tk)

    def make_chain(q_rows, head, bufs):
        s_a, s_b, p_a, p_b, al_a, al_b, m_ref, l_ref, acc_ref, pv_ref = bufs

        def scores(s_ref, j, rows):
            s_ref[slice(*rows)] = lax.dot_general(
                q_rows(*rows), k_ref[0, head, key_slice(j), :], (((1,), (1,)), ((), ())),
                preferred_element_type=F32)

        def softmax(s_ref, p_ref, al_ref, rows, diag=None):
            for r0 in range(rows[0], rows[1], SOFTMAX_ROWS):
                blk = slice(r0, r0 + SOFTMAX_ROWS)
                s = s_ref[blk]
                if diag is not None:
                    q0 = r0 - diag * tk
                    assert q0 + SOFTMAX_ROWS > 0
                    if q0 < tk - 1:
                        row = lax.broadcasted_iota(jnp.int32, s.shape, 0) + q0
                        col = lax.broadcasted_iota(jnp.int32, s.shape, 1)
                        s = jnp.where(col <= row, s, MASK_VALUE)
                m_old = m_ref[blk]
                m_blk = jnp.max(s, axis=1, keepdims=True)
                m_new = jnp.maximum(m_old, jnp.broadcast_to(m_blk, m_old.shape))
                alpha = jnp.exp2(m_old - m_new)
                p_lanes = None
                for c in range(0, tk, HEAD_W):
                    p = jnp.exp2(s[:, c:c + HEAD_W] - m_new)
                    p_ref[blk, c:c + HEAD_W] = p.astype(BF16)
                    p_lanes = p if p_lanes is None else p_lanes + p
                l_ref[blk] = alpha * l_ref[blk] + p_lanes
                m_ref[blk] = m_new
                al_ref[blk] = alpha

        def weighted_values(p_ref, al_ref, j, rows, own_buffer=False):
            blk = slice(*rows)
            pv = _dot(p_ref[blk], v_ref[0, head, key_slice(j), :])
            if own_buffer:
                pv_ref[blk] = pv
                pv = pv_ref[blk]
            acc_ref[blk] = al_ref[blk] * acc_ref[blk] + pv

        def diagonal():
            m_ref[...] = jnp.full(m_ref.shape, MASK_VALUE, F32)
            l_ref[...] = jnp.zeros(l_ref.shape, F32)
            acc_ref[...] = jnp.zeros(acc_ref.shape, F32)
            scores(s_a, d0 + 1, upper_rows)
            scores(s_b, d0, all_rows)
            softmax(s_a, p_a, al_a, upper_rows, diag=1)
            weighted_values(p_a, al_a, d0 + 1, upper_rows)
            softmax(s_b, p_b, al_b, all_rows, diag=0)
            scores(s_a, 0, all_rows)

        def pair(i, pending, prefetch=True):
            weighted_values(p_b, al_b, pending, all_rows, own_buffer=prefetch)
            scores(s_b, 2 * i + 1, all_rows)
            softmax(s_a, p_a, al_a, all_rows)
            if prefetch:
                scores(s_a, 2 * i + 2, all_rows)
            weighted_values(p_a, al_a, 2 * i, all_rows, own_buffer=prefetch)
            softmax(s_b, p_b, al_b, all_rows)
            return 2 * i + 1

        def full_tiles():
            pending = lax.fori_loop(0, qi - 1, pair, d0)

            @pl.when(qi > 0)
            def _():
                pair(qi - 1, pending, prefetch=False)

        def finish():
            weighted_values(p_b, al_b, jnp.where(qi > 0, 2 * qi - 1, d0), all_rows)
            return acc_ref[...] / jnp.sum(l_ref[...], axis=1, keepdims=True)

        return diagonal, full_tiles, finish

    n_bufs = 10
    n_heads = q_ref.shape[1]
    lane = lax.broadcasted_iota(jnp.int32, (tq, HEAD_W), 1)
    chains = []
    for h in range(n_heads):
        if diff:
            for c, keep in enumerate((lane < DIFF_HEAD_DIM, lane >= DIFF_HEAD_DIM)):
                n = 2 * h + c
                qs_ref = scratch[2 * n_heads * n_bufs + n]
                qs_ref[...] = jnp.where(keep, q_ref[0, h], jnp.zeros((tq, HEAD_W), BF16))
                chains.append(make_chain(lambda lo, hi, r=qs_ref: r[lo:hi], h,
                                         scratch[n * n_bufs:(n + 1) * n_bufs]))
        else:
            chains.append(make_chain(lambda lo, hi, h=h: q_ref[0, h, lo:hi], h,
                                     scratch[h * n_bufs:(h + 1) * n_bufs]))

    for diagonal, _, _ in chains:
        diagonal()
    for _, full_tiles, _ in chains:
        full_tiles()
    outs = [finish() for _, _, finish in chains]

    if diff:
        lq1_ref, lk1_ref, lq2_ref, lk2_ref, sg_ref = extra
        lam = (jnp.exp(jnp.sum(lq1_ref[...] * lk1_ref[...], axis=-1, keepdims=True))
               - jnp.exp(jnp.sum(lq2_ref[...] * lk2_ref[...], axis=-1, keepdims=True))
               + lambda_init)
        outs = [_rms(outs[2 * h] - lam * outs[2 * h + 1], sg_ref[...]) * (1.0 - lambda_init)
                for h in range(n_heads)]
    for h, o in enumerate(outs):
        o_ref[0, h] = o.astype(BF16)


def _const_spec(shape):
    zeros = (0,) * len(shape)
    return pl.BlockSpec(shape, lambda *_: zeros, pipeline_mode=pl.Buffered(1))


def _token_specs(b_s, widths):
    del b_s
    return [pl.BlockSpec((1, TOKEN_TILE, w), lambda b, i: (b, i, 0)) for w in widths]


def _head_spec(width):
    return pl.BlockSpec((1, N_HEADS, TOKEN_TILE, width), lambda b, i: (b, 0, i, 0))


def _params():
    return pltpu.CompilerParams(dimension_semantics=("arbitrary", "arbitrary"),
                                vmem_limit_bytes=VMEM_LIMIT_BYTES)


def _layer_spec(shape, layer):
    return pl.BlockSpec((None,) + shape, lambda *_: (layer, 0, 0), pipeline_mode=pl.Buffered(1))


def _ffn_specs(layer):
    return [_const_spec((1, D_MODEL)), _layer_spec((D_MODEL, D_FF), layer),
            _layer_spec((D_MODEL, D_FF), layer), _layer_spec((D_FF, D_MODEL), layer)]


def _pre_diff(x, phases, ffn, mix_g, w_in):
    b, s, _ = x.shape
    head = jax.ShapeDtypeStruct((b, N_HEADS, s, HEAD_W), BF16)
    return pl.pallas_call(
        _pre_diff_kernel,
        grid=(b, s // TOKEN_TILE),
        in_specs=_token_specs((b, s), [D_MODEL, ROPE_DIM // 2, ROPE_DIM // 2]) + _ffn_specs(ffn[0])
        + [_const_spec((1, D_MODEL)), _const_spec((D_MODEL, 3 * D_MODEL))],
        out_specs=_token_specs((b, s), [D_MODEL]) + [_head_spec(HEAD_W)] * 3,
        out_shape=[jax.ShapeDtypeStruct(x.shape, F32), head, head, head],
        compiler_params=_params(),
        name="pre_diff",
    )(x, *phases, *ffn[1:], mix_g, w_in)


def _pre_mla(x, phases, ffn, mix_g, w_in, q_norm, w_q, kv_norm, w_kv):
    b, s, _ = x.shape
    head = jax.ShapeDtypeStruct((b, N_HEADS, s, HEAD_W), BF16)
    head_qk = jax.ShapeDtypeStruct((b, N_HEADS, s, MLA_QK_W), BF16)
    return pl.pallas_call(
        _pre_mla_kernel,
        grid=(b, s // TOKEN_TILE),
        in_specs=_token_specs((b, s), [D_MODEL, ROPE_DIM // 2, ROPE_DIM // 2]) + _ffn_specs(ffn[0])
        + [_const_spec((1, D_MODEL)), _const_spec(w_in.shape), _const_spec(q_norm.shape),
           _const_spec(w_q.shape), _const_spec(kv_norm.shape), _const_spec(w_kv.shape)],
        out_specs=_token_specs((b, s), [D_MODEL])
        + [_head_spec(MLA_QK_W), _head_spec(MLA_QK_W), _head_spec(HEAD_W)],
        out_shape=[jax.ShapeDtypeStruct(x.shape, F32), head_qk, head_qk, head],
        compiler_params=_params(),
        name="pre_mla",
    )(x, *phases, *ffn[1:], mix_g, w_in, q_norm, w_q, kv_norm, w_kv)


def _post(x, o, w_out, ffn, final_gain=None):
    b, s, _ = x.shape
    final_norm = final_gain is not None
    extra = [final_gain] if final_norm else []
    return pl.pallas_call(
        functools.partial(_post_kernel, final_norm=final_norm),
        grid=(b, s // TOKEN_TILE),
        in_specs=_token_specs((b, s), [D_MODEL]) + [_head_spec(HEAD_W), _const_spec(w_out.shape)]
        + _ffn_specs(ffn[0]) + [_const_spec((1, D_MODEL))] * len(extra),
        out_specs=_token_specs((b, s), [D_MODEL])[0],
        out_shape=jax.ShapeDtypeStruct(x.shape, F32),
        compiler_params=_params(),
        name="post_final" if final_norm else "post",
    )(x, o, w_out, *ffn[1:], *extra)


def _attention(q, k, v, diff_params=None, lambda_init=0.0):
    b, h, s, dqk = q.shape
    tq, tk = ATTN_Q_TILE, ATTN_K_TILE
    diff = diff_params is not None
    extra = list(diff_params) if diff else []
    heads = 2
    n_chains = heads * (2 if diff else 1)

    def chain_scratch():
        return ([pltpu.VMEM((tq, tk), F32), pltpu.VMEM((tq, tk), F32),
                 pltpu.VMEM((tq, tk), BF16), pltpu.VMEM((tq, tk), BF16)]
                + [pltpu.VMEM((tq, HEAD_W), F32) for _ in range(6)])

    return pl.pallas_call(
        functools.partial(_attn_kernel, tq=tq, tk=tk, diff=diff, lambda_init=lambda_init),
        grid=(b, h // heads, s // tq),
        in_specs=[pl.BlockSpec((1, heads, tq, dqk), lambda bi, hi, qi: (bi, hi, qi, 0)),
                  pl.BlockSpec((1, heads, s, dqk), lambda bi, hi, qi: (bi, hi, 0, 0)),
                  pl.BlockSpec((1, heads, s, HEAD_W), lambda bi, hi, qi: (bi, hi, 0, 0))]
        + [pl.BlockSpec(e.shape, lambda bi, hi, qi: (0, 0)) for e in extra],
        out_specs=pl.BlockSpec((1, heads, tq, HEAD_W), lambda bi, hi, qi: (bi, hi, qi, 0)),
        out_shape=jax.ShapeDtypeStruct((b, h, s, HEAD_W), BF16),
        scratch_shapes=[buf for _ in range(n_chains) for buf in chain_scratch()]
        + ([pltpu.VMEM((tq, dqk), BF16) for _ in range(n_chains)] if diff else []),
        compiler_params=pltpu.CompilerParams(
            dimension_semantics=("arbitrary", "arbitrary", "arbitrary"),
            vmem_limit_bytes=VMEM_LIMIT_BYTES),
        name="attn_diff" if diff else "attn_mla",
    )(q, k, v, *extra)


def _phase_kernel(pos_ref, freq_ref, cos_ref, sin_ref):
    ang = pos_ref[...] * freq_ref[...]
    cos_ref[...] = jnp.cos(ang)
    sin_ref[...] = jnp.sin(ang)


def _rope_phases(positions):
    b, s = positions.shape
    half = ROPE_DIM // 2
    per_row = HEAD_W // half
    rows = b * s // per_row
    inv_freq = ROPE_THETA ** (-jnp.arange(0, ROPE_DIM, 2, dtype=F32) / ROPE_DIM)
    pos = jnp.repeat(positions.astype(F32).reshape(rows, per_row), half, axis=1)
    spec = pl.BlockSpec((PHASE_ROWS, HEAD_W), lambda i: (i, 0))
    out = jax.ShapeDtypeStruct((rows, HEAD_W), F32)
    cos, sin = pl.pallas_call(
        _phase_kernel,
        grid=(rows // PHASE_ROWS,),
        in_specs=[spec, pl.BlockSpec((1, HEAD_W), lambda i: (0, 0))],
        out_specs=[spec, spec],
        out_shape=[out, out],
        name="rope_phases",
    )(pos, jnp.tile(inv_freq, per_row).reshape(1, HEAD_W))
    return cos.reshape(b, s, half), sin.reshape(b, s, half)


def _row(v):
    return v.reshape(1, -1).astype(F32)


def kernel(x, positions, ffn1_norm, ffn1_w_gate, ffn1_w_up, ffn1_w_down, mix_norm, ffn2_norm, ffn2_w_gate, ffn2_w_up, ffn2_w_down, diff_w_in, diff_lambda_q1, diff_lambda_k1, diff_lambda_q2, diff_lambda_k2, diff_sub_norm, diff_w_out, mla_w_in, mla_q_norm, mla_w_q_up, mla_kv_norm, mla_w_kv_up, mla_w_out, final_norm):
    depth = ffn1_norm.shape[0]
    assert depth == 2 and diff_w_in.shape[0] == 1 and mla_w_in.shape[0] == 1
    phases = _rope_phases(positions)

    def ffn(norm, wg, wu, wd):
        wg, wu, wd = wg.astype(BF16), wu.astype(BF16), wd.astype(BF16)
        return [(i, _row(norm[i]), wg, wu, wd) for i in range(depth)]

    ffn1 = ffn(ffn1_norm, ffn1_w_gate, ffn1_w_up, ffn1_w_down)
    ffn2 = ffn(ffn2_norm, ffn2_w_gate, ffn2_w_up, ffn2_w_down)

    lambda_init = 0.8 - 0.6 * math.exp(-0.3 * 0)
    x, q, k, v = _pre_diff(x, phases, ffn1[0], _row(mix_norm[0]), diff_w_in[0].astype(BF16))
    diff_params = (_row(diff_lambda_q1[0]), _row(diff_lambda_k1[0]), _row(diff_lambda_q2[0]),
                   _row(diff_lambda_k2[0]), _row(diff_sub_norm[0]))
    o = _attention(q, k, v, diff_params, lambda_init)
    x = _post(x, o, diff_w_out[0].astype(BF16), ffn2[0])

    w_in = jnp.pad(mla_w_in[0], ((0, 0), (0, HEAD_W - MLA_ROPE))).astype(BF16)
    w_q = mla_w_q_up[0].reshape(MLA_Q_RANK, N_HEADS, MLA_NOPE + MLA_ROPE)
    w_q = jnp.pad(w_q, ((0, 0), (0, 0), (0, MLA_QK_W - MLA_NOPE - MLA_ROPE)))
    w_q = w_q.reshape(MLA_Q_RANK, N_HEADS * MLA_QK_W).astype(BF16)
    x, q, k, v = _pre_mla(x, phases, ffn1[1], _row(mix_norm[1]), w_in, _row(mla_q_norm[0]), w_q,
                          _row(mla_kv_norm[0]), mla_w_kv_up[0].astype(BF16))
    o = _attention(q, k, v)
    return _post(x, o, mla_w_out[0].astype(BF16), ffn2[1], _row(final_norm))
```
